```python
import math
import jax
import jax.numpy as jnp
from jax import lax
import numpy as np

D_MODEL = 2048
BATCH = 16
SEQ = 2048
DEPTH = 4

GRID_W = 64
CTX_LEN = 256
N_MIXERS = 4
NORM_EPS = 1e-6
ROPE_THETA = 10000.0
Q_BLOCK = 128
N_MOD = 6
D_FF = -(-(8 * D_MODEL) // (3 * 256)) * 256

A_HEAD_DIM = 128
A_N_HEADS = D_MODEL // A_HEAD_DIM
A_N_KV_HEADS = max(A_N_HEADS // 4, 1)
A_GROUP = A_N_HEADS // A_N_KV_HEADS
A_Q_WIDTH = A_N_HEADS * A_HEAD_DIM
A_KV_WIDTH = A_N_KV_HEADS * A_HEAD_DIM

B_HEAD_DIM = 128
B_N_HEADS = D_MODEL // B_HEAD_DIM
B_N_PROJ = 5
B_CHUNK = 32

C_D_INNER = 2 * D_MODEL
C_HEAD_DIM = 64
C_N_HEADS = C_D_INNER // C_HEAD_DIM
C_N_GROUPS = 8
C_HEADS_PER_GROUP = C_N_HEADS // C_N_GROUPS
C_D_STATE = 128
C_GN = C_N_GROUPS * C_D_STATE
C_CONV = 3
C_CONV_CH = C_D_INNER + 2 * C_GN
C_IN_WIDTH = 2 * C_D_INNER + 2 * C_GN + 2 * C_N_HEADS
C_CHUNK = 64
C_NORM_GROUP = C_D_INNER // C_N_GROUPS

D_N_HEADS = 16
D_HEAD_DIM = D_MODEL // D_N_HEADS // 2
D_V_DIM = 2 * D_HEAD_DIM

kernel_name = "hybrid_interleaved_dit_block"
F32 = jnp.float32


def rms_norm(x, g, eps=NORM_EPS):
    xf = x.astype(F32)
    y = xf * lax.rsqrt(jnp.mean(xf * xf, axis=-1, keepdims=True) + eps)
    return (y * g.astype(F32)).astype(x.dtype)


def axial_rope(n, head_dim):
    rows = n // GRID_W
    row = jnp.repeat(jnp.arange(rows, dtype=F32), GRID_W)
    col = jnp.tile(jnp.arange(GRID_W, dtype=F32), rows)
    axis_dim = head_dim // 2
    inv_freq = ROPE_THETA ** (-jnp.arange(0, axis_dim, 2, dtype=F32) / axis_dim)
    ang = jnp.concatenate([row[:, None] * inv_freq, col[:, None] * inv_freq], axis=-1)
    return jnp.cos(ang), jnp.sin(ang)


def apply_rope(x, cos, sin):
    xf = x.astype(F32)
    x1, x2 = xf[..., 0::2], xf[..., 1::2]
    out = jnp.stack([x1 * cos - x2 * sin, x1 * sin + x2 * cos], axis=-1)
    return out.reshape(x.shape).astype(x.dtype)


def sweep_query_blocks(attend, q):
    b, n = q.shape[:2]
    nb = n // Q_BLOCK
    qb = jnp.moveaxis(q.reshape(b, nb, Q_BLOCK, *q.shape[2:]), 1, 0)
    out = lax.map(attend, qb)
    return jnp.moveaxis(out, 0, 1).reshape(b, n, *out.shape[3:])


def flip_seq(a, direction):
    return jnp.flip(a, axis=1) if direction else a


def to_chunks(a, chunk):
    b, n = a.shape[:2]
    return jnp.moveaxis(a.reshape(b, n // chunk, chunk, *a.shape[2:]), 1, 0)


def from_chunks(a):
    nc, b, chunk = a.shape[:3]
    return jnp.moveaxis(a, 0, 1).reshape(b, nc * chunk, *a.shape[3:])


def adaln(cond, w_mod, b_mod):
    m = jax.nn.silu(cond) @ w_mod + b_mod
    return jnp.split(m, N_MOD, axis=-1)


def modulate(h, shift, scale):
    return h * (1 + scale) + shift


def swiglu(h, w13, w2):
    gate, up = jnp.split(h @ w13, 2, axis=-1)
    return (jax.nn.silu(gate) * up) @ w2


def gqa_attend(qb, k, v, scale):
    s = jnp.einsum('bqhgd,bkhd->bhgqk', qb, k, preferred_element_type=F32) * scale
    p = jax.nn.softmax(s, axis=-1)
    return jnp.einsum('bhgqk,bkhe->bqhge', p.astype(v.dtype), v)


def mixer_gqa(h_lat, h_ctx, w_in, q_g, k_g, w_out, need_ctx):
    b, n, _ = h_lat.shape
    cos, sin = axial_rope(n, A_HEAD_DIM)

    def project(h):
        lead = h.shape[:2]
        q, k, v = jnp.split(h @ w_in, [A_Q_WIDTH, A_Q_WIDTH + A_KV_WIDTH], axis=-1)
        q = rms_norm(q.reshape(*lead, A_N_KV_HEADS, A_GROUP, A_HEAD_DIM), q_g)
        k = rms_norm(k.reshape(*lead, A_N_KV_HEADS, A_HEAD_DIM), k_g)
        return q, k, v.reshape(*lead, A_N_KV_HEADS, A_HEAD_DIM)

    q_l, k_l, v_l = project(h_lat)
    q_c, k_c, v_c = project(h_ctx)
    q_l = apply_rope(q_l, cos[:, None, None], sin[:, None, None])
    k_l = apply_rope(k_l, cos[:, None], sin[:, None])
    k_all = jnp.concatenate([k_l, k_c], axis=1)
    v_all = jnp.concatenate([v_l, v_c], axis=1)
    scale = A_HEAD_DIM ** -0.5
    o_l = sweep_query_blocks(lambda qb: gqa_attend(qb, k_all, v_all, scale), q_l)
    out_l = o_l.reshape(b, n, A_Q_WIDTH) @ w_out
    out_c = None
    if need_ctx:
        o_c = sweep_query_blocks(lambda qb: gqa_attend(qb, k_c, v_c, scale), q_c)
        out_c = o_c.reshape(*h_ctx.shape[:2], A_Q_WIDTH) @ w_out
    return out_l, out_c


def hgrn_lower_bound(lb_logits, layer):
    cum = jnp.cumsum(jax.nn.softmax(lb_logits.astype(F32), axis=1), axis=1)
    return cum[:, layer] - cum[:, 0]


def hgrn_scan(q, k, v, log_f, s0):
    tri = jnp.tril(jnp.ones((B_CHUNK, B_CHUNK), dtype=bool))

    def step(state, inp):
        qc, kc, vc, lf = inp
        cum = jnp.cumsum(lf, axis=1)
        last = cum[:, -1]
        q_dec = qc * jnp.exp(cum)
        k_inv = kc * jnp.exp(-cum)
        k_end = kc * jnp.exp(last[:, None] - cum)
        att = jnp.where(tri, jnp.einsum('bthk,bshk->bhts', q_dec, k_inv), 0.0)
        o = (jnp.einsum('bhts,bshv->bthv', att, vc)
             + jnp.einsum('bthk,bhkv->bthv', q_dec, state))
        state = state * jnp.exp(last)[..., None] + jnp.einsum('bshk,bshv->bhkv', k_end, vc)
        return state, o

    xs = tuple(to_chunks(a, B_CHUNK) for a in (q, k, v, log_f))
    s_final, o = lax.scan(step, s0, xs)
    return from_chunks(o), s_final


def mixer_hgrn(h_lat, h_ctx, w_in, lb, out_g, w_out, need_ctx):
    def project(h):
        lead = h.shape[:2]
        heads = lambda a: a.reshape(*lead, B_N_HEADS, B_HEAD_DIM)
        q, f_fw, f_bw, i, g = jnp.split(h @ w_in, B_N_PROJ, axis=-1)
        q = heads(jax.nn.silu(q)).astype(F32)
        v = heads(i).astype(F32)
        forget = [heads(lb[d] + (1 - lb[d]) * jax.nn.sigmoid(fr.astype(F32)))
                  for d, fr in enumerate((f_fw, f_bw))]
        return q, v, forget, g

    q_l, v_l, f_l, g_l = project(h_lat)
    q_c, v_c, f_c, g_c = project(h_ctx)
    b = h_lat.shape[0]
    s0 = jnp.zeros((b, B_N_HEADS, B_HEAD_DIM, B_HEAD_DIM), F32)
    o_l, o_c = 0.0, 0.0
    for d in range(2):
        oc, s_ctx = hgrn_scan(flip_seq(q_c, d), flip_seq(1 - f_c[d], d), flip_seq(v_c, d),
                              flip_seq(jnp.log(f_c[d]), d), s0)
        ol, _ = hgrn_scan(flip_seq(q_l, d), flip_seq(1 - f_l[d], d), flip_seq(v_l, d),
                          flip_seq(jnp.log(f_l[d]), d), s_ctx)
        o_l = o_l + flip_seq(ol, d)
        o_c = o_c + flip_seq(oc, d)

    def readout(o, g):
        o = rms_norm(o, out_g).reshape(*g.shape) * jax.nn.silu(g.astype(F32))
        return o.astype(g.dtype) @ w_out

    return readout(o_l, g_l), (readout(o_c, g_c) if need_ctx else None)


def depthwise_conv(u, w, bias):
    pad = C_CONV // 2
    out = lax.conv_general_dilated(u, w[:, None, :], window_strides=(1,), padding=[(pad, pad)],
                                   dimension_numbers=('NWC', 'WIO', 'NWC'),
                                   feature_group_count=u.shape[-1])
    return out + bias


def ssd_scan(xs, dt, a, bm, cm, h0):
    tri = jnp.tril(jnp.ones((C_CHUNK, C_CHUNK), dtype=bool))

    def step(h, inp):
        xc, dtc, bc, cc = inp
        cum = jnp.cumsum(dtc * a, axis=1)
        last = cum[:, -1]
        seg = cum[:, :, None] - cum[:, None]
        decay = jnp.exp(jnp.where(tri[None, :, :, None, None], seg, -jnp.inf))
        cb = jnp.einsum('btgn,bsgn->btsg', cc, bc)
        w = cb[..., None] * decay * dtc[:, None]
        y = jnp.einsum('btsgj,bsgjp->btgjp', w, xc)
        y = y + jnp.einsum('btgn,bgjpn->btgjp', cc, h) * jnp.exp(cum)[..., None]
        h = (h * jnp.exp(last)[..., None, None]
             + jnp.einsum('bsgj,bsgn,bsgjp->bgjpn', jnp.exp(last[:, None] - cum) * dtc, bc, xc))
        return h, y

    inps = tuple(to_chunks(t, C_CHUNK) for t in (xs, dt, bm, cm))
    h_final, y = lax.scan(step, h0, inps)
    return from_chunks(y), h_final


def mixer_ssd(h_lat, h_ctx, w_in, conv_w, conv_b, dt_bias, a_log, d_skip, norm_g, w_out, need_ctx):
    def project(h):
        lead = h.shape[:2]
        z, xbc, dt = jnp.split(h @ w_in, [C_D_INNER, 2 * C_D_INNER + 2 * C_GN], axis=-1)
        xbc = jax.nn.silu(depthwise_conv(xbc, conv_w, conv_b))
        xs, bm, cm = jnp.split(xbc, [C_D_INNER, C_D_INNER + C_GN], axis=-1)
        xs = xs.reshape(*lead, C_N_GROUPS, C_HEADS_PER_GROUP, C_HEAD_DIM).astype(F32)
        bm = bm.reshape(*lead, C_N_GROUPS, C_D_STATE).astype(F32)
        cm = cm.reshape(*lead, C_N_GROUPS, C_D_STATE).astype(F32)
        dt = jax.nn.softplus(dt.astype(F32).reshape(*lead, 2, C_N_HEADS) + dt_bias.astype(F32))
        return z, xs, bm, cm, dt.reshape(*lead, 2, C_N_GROUPS, C_HEADS_PER_GROUP)

    z_l, x_l, b_l, c_l, dt_l = project(h_lat)
    z_c, x_c, b_c, c_c, dt_c = project(h_ctx)
    a = -jnp.exp(a_log.astype(F32)).reshape(2, C_N_GROUPS, C_HEADS_PER_GROUP)
    d_heads = d_skip.astype(F32).reshape(C_N_GROUPS, C_HEADS_PER_GROUP)[..., None]
    b = h_lat.shape[0]
    h0 = jnp.zeros((b, C_N_GROUPS, C_HEADS_PER_GROUP, C_HEAD_DIM, C_D_STATE), F32)
    y_l, y_c = d_heads * x_l, d_heads * x_c
    for d in range(2):
        yc, h_ctx_state = ssd_scan(flip_seq(x_c, d), flip_seq(dt_c[:, :, d], d), a[d],
                                   flip_seq(b_c, d), flip_seq(c_c, d), h0)
        yl, _ = ssd_scan(flip_seq(x_l, d), flip_seq(dt_l[:, :, d], d), a[d],
                         flip_seq(b_l, d), flip_seq(c_l, d), h_ctx_state)
        y_l = y_l + flip_seq(yl, d)
        y_c = y_c + flip_seq(yc, d)

    def readout(y, z):
        lead = z.shape[:2]
        y = y.reshape(*lead, C_D_INNER) * jax.nn.silu(z.astype(F32))
        y = rms_norm(y.reshape(*lead, C_N_GROUPS, C_NORM_GROUP),
                     norm_g.reshape(C_N_GROUPS, C_NORM_GROUP))
        return y.reshape(*lead, C_D_INNER).astype(z.dtype) @ w_out

    return readout(y_l, z_l), (readout(y_c, z_c) if need_ctx else None)


def mixer_diff(h_lat, h_ctx, w_in, lam_p, subln_g, w_out, lambda_init, need_ctx):
    b, n, _ = h_lat.shape
    cos, sin = axial_rope(n, D_HEAD_DIM)

    def project(h):
        lead = h.shape[:2]
        q, k, v = jnp.split(h @ w_in, 3, axis=-1)
        return (q.reshape(*lead, D_N_HEADS, 2, D_HEAD_DIM),
                k.reshape(*lead, D_N_HEADS, 2, D_HEAD_DIM),
                v.reshape(*lead, D_N_HEADS, D_V_DIM))

    q_l, k_l, v_l = project(h_lat)
    q_c, k_c, v_c = project(h_ctx)
    q_l = apply_rope(q_l, cos[:, None, None], sin[:, None, None])
    k_l = apply_rope(k_l, cos[:, None, None], sin[:, None, None])
    k_all = jnp.concatenate([k_l, k_c], axis=1)
    v_all = jnp.concatenate([v_l, v_c], axis=1)
    lp = lam_p.astype(F32)
    lam = jnp.exp(jnp.sum(lp[0] * lp[1])) - jnp.exp(jnp.sum(lp[2] * lp[3])) + lambda_init
    scale = D_HEAD_DIM ** -0.5

    def attend(qb, k, v):
        s = jnp.einsum('bqhcd,bkhcd->bhcqk', qb, k, preferred_element_type=F32) * scale
        p = jax.nn.softmax(s, axis=-1)
        diff = p[:, :, 0] - lam * p[:, :, 1]
        return jnp.einsum('bhqk,bkhe->bqhe', diff.astype(v.dtype), v)

    def readout(o):
        o = rms_norm(o, subln_g) * (1 - lambda_init)
        return o.reshape(*o.shape[:2], D_N_HEADS * D_V_DIM) @ w_out

    out_l = readout(sweep_query_blocks(lambda qb: attend(qb, k_all, v_all), q_l))
    out_c = readout(sweep_query_blocks(lambda qb: attend(qb, k_c, v_c), q_c)) if need_ctx else None
    return out_l, out_c


def setup_inputs(seed: int = 0) -> dict:
    key = jax.random.key(seed)
    keys = iter(jax.random.split(key, 48))
    D = D_MODEL
    nrm = lambda shape, s: jax.random.normal(next(keys), shape, F32) * s
    gain = lambda shape: 1.0 + nrm(shape, 0.05)
    n_a, n_b, n_c, n_d = (len(range(k, DEPTH, N_MIXERS)) for k in range(N_MIXERS))
    dt0 = jnp.exp(jax.random.uniform(next(keys), (n_c, 2, C_N_HEADS), F32,
                                     math.log(1e-3), math.log(1e-1)))
    return {
        "x": nrm((BATCH, SEQ, D), 1.0),
        "c": nrm((BATCH, D), 1.0),
        "ctx": nrm((BATCH, CTX_LEN, D), 1.0),
        "c_ctx": nrm((D,), 1.0),
        "w_mod": nrm((DEPTH, D, N_MOD * D), 0.5 * D ** -0.5),
        "b_mod": nrm((DEPTH, N_MOD * D), 0.02),
        "norm_g": gain((DEPTH, 4, D)),
        "ffn_w13": nrm((DEPTH, D, 2 * D_FF), D ** -0.5),
        "ffn_w2": nrm((DEPTH, D_FF, D), D_FF ** -0.5),
        "attn_w_in": nrm((n_a, D, A_Q_WIDTH + 2 * A_KV_WIDTH), D ** -0.5),
        "attn_q_g": gain((n_a, A_HEAD_DIM)),
        "attn_k_g": gain((n_a, A_HEAD_DIM)),
        "attn_w_out": nrm((n_a, A_Q_WIDTH, D), A_Q_WIDTH ** -0.5),
        "hgrn_w_in": nrm((n_b, D, B_N_PROJ * D), D ** -0.5),
        "hgrn_lb_logits": 1.0 + nrm((2, DEPTH, D), 0.1),
        "hgrn_out_g": gain((n_b, B_HEAD_DIM)),
        "hgrn_w_out": nrm((n_b, D, D), D ** -0.5),
        "ssd_w_in": nrm((n_c, D, C_IN_WIDTH), D ** -0.5),
        "ssd_conv_w": nrm((n_c, C_CONV, C_CONV_CH), C_CONV ** -0.5),
        "ssd_conv_b": nrm((n_c, C_CONV_CH), 0.02),
        "ssd_dt_bias": dt0 + jnp.log(-jnp.expm1(-dt0)),
        "ssd_a_log": jnp.log(jax.random.uniform(next(keys), (n_c, 2, C_N_HEADS), F32, 1.0, 16.0)),
        "ssd_d": gain((n_c, C_N_HEADS)),
        "ssd_norm_g": gain((n_c, C_D_INNER)),
        "ssd_w_out": nrm((n_c, C_D_INNER, D), C_D_INNER ** -0.5),
        "diff_w_in": nrm((n_d, D, 3 * D), D ** -0.5),
        "diff_lambda": nrm((n_d, 4, D_HEAD_DIM), 0.1),
        "diff_subln_g": gain((n_d, D_V_DIM)),
        "diff_w_out": nrm((n_d, D_N_HEADS * D_V_DIM, D), D ** -0.5),
    }


def reference(x, c, ctx, c_ctx, w_mod, b_mod, norm_g, ffn_w13, ffn_w2,
              attn_w_in, attn_q_g, attn_k_g, attn_w_out,
              hgrn_w_in, hgrn_lb_logits, hgrn_out_g, hgrn_w_out,
              ssd_w_in, ssd_conv_w, ssd_conv_b, ssd_dt_bias, ssd_a_log, ssd_d, ssd_norm_g, ssd_w_out,
              diff_w_in, diff_lambda, diff_subln_g, diff_w_out):
    x_lat, x_ctx = x, ctx
    for layer in range(DEPTH):
        kind, j = layer % N_MIXERS, layer // N_MIXERS
        need_ctx = layer < DEPTH - 1
        mod_l = [m[:, None, :] for m in adaln(c, w_mod[layer], b_mod[layer])]
        mod_c = adaln(c_ctx, w_mod[layer], b_mod[layer])
        g = norm_g[layer]

        h_lat = modulate(rms_norm(x_lat, g[0]), mod_l[0], mod_l[1])
        h_ctx = modulate(rms_norm(x_ctx, g[0]), mod_c[0], mod_c[1])
        if kind == 0:
            m_lat, m_ctx = mixer_gqa(h_lat, h_ctx, attn_w_in[j], attn_q_g[j], attn_k_g[j],
                                     attn_w_out[j], need_ctx)
        elif kind == 1:
            lb = hgrn_lower_bound(hgrn_lb_logits, layer)
            m_lat, m_ctx = mixer_hgrn(h_lat, h_ctx, hgrn_w_in[j], lb, hgrn_out_g[j],
                                      hgrn_w_out[j], need_ctx)
        elif kind == 2:
            m_lat, m_ctx = mixer_ssd(h_lat, h_ctx, ssd_w_in[j], ssd_conv_w[j], ssd_conv_b[j],
                                     ssd_dt_bias[j], ssd_a_log[j], ssd_d[j], ssd_norm_g[j],
                                     ssd_w_out[j], need_ctx)
        else:
            lambda_init = 0.8 - 0.6 * math.exp(-0.3 * layer)
            m_lat, m_ctx = mixer_diff(h_lat, h_ctx, diff_w_in[j], diff_lambda[j], diff_subln_g[j],
                                      diff_w_out[j], lambda_init, need_ctx)
        x_lat = x_lat + mod_l[2] * rms_norm(m_lat, g[1])

        f_lat = swiglu(modulate(rms_norm(x_lat, g[2]), mod_l[3], mod_l[4]), ffn_w13[layer], ffn_w2[layer])
        x_lat = x_lat + mod_l[5] * rms_norm(f_lat, g[3])
        if need_ctx:
            x_ctx = x_ctx + mod_c[2] * rms_norm(m_ctx, g[1])
            f_ctx = swiglu(modulate(rms_norm(x_ctx, g[2]), mod_c[3], mod_c[4]), ffn_w13[layer], ffn_w2[layer])
            x_ctx = x_ctx + mod_c[5] * rms_norm(f_ctx, g[3])
    return x_lat
```

```python
import functools
import math

import jax
import jax.numpy as jnp
from jax import lax
from jax.experimental import pallas as pl
from jax.experimental.pallas import tpu as pltpu

F32 = jnp.float32
BF16 = jnp.bfloat16

NORM_EPS = 1e-6
ROPE_THETA = 10000.0
GRID_W = 64
N_MOD = 6
HEAD_128 = 128
GQA_GROUP = 4
HGRN_CHUNK = 32
HGRN_N_PROJ = 5
SSD_HEAD_DIM = 64
SSD_GROUPS = 8
SSD_STATE = 128
DIFF_HEADS = 16

LANES = 128
VMEM_LIMIT_BYTES = 56 * 2**20
TM_IN = 512
TM_OUT = 512
TK_OUT = 512
TQ_GQA = 256
TQ_DIFF = 512
HGRN_BLOCK = 256
HGRN_HEADS_PER_STEP = 4
SSD_CHUNK = 128


def _cparams(n_axes):
    return pltpu.CompilerParams(dimension_semantics=("arbitrary",) * n_axes,
                                vmem_limit_bytes=VMEM_LIMIT_BYTES)


def _silu(x):
    return x / (1.0 + jnp.exp(-x))


def _dot(a, b):
    return jnp.dot(a, b, preferred_element_type=F32)


def _dot_nt(a, b):
    return lax.dot_general(a, b, (((1,), (1,)), ((), ())), preferred_element_type=F32)


def _dot_tn(a, b):
    return lax.dot_general(a, b, (((0,), (0,)), ((), ())), preferred_element_type=F32)


def _dot_exact_lhs(m01, x):
    hi = x.astype(BF16)
    r1 = x - hi.astype(F32)
    mid = r1.astype(BF16)
    lo = (r1 - mid.astype(F32)).astype(BF16)
    return _dot(m01, hi) + _dot(m01, mid) + _dot(m01, lo)


def _adaln_kernel(c_ref, w_ref, b_ref, o_ref):
    s = _silu(c_ref[...]).astype(BF16)
    o_ref[0] = _dot(s, w_ref[0].astype(BF16)) + b_ref[0]


def adaln_all(c_all, w_mod, b_mod):
    n_layers, d, n = w_mod.shape
    r = c_all.shape[0]
    tn = 1024
    return pl.pallas_call(
        _adaln_kernel,
        grid=(n_layers, n // tn),
        in_specs=[pl.BlockSpec((r, d), lambda l, j: (0, 0)),
                  pl.BlockSpec((1, d, tn), lambda l, j: (l, 0, j)),
                  pl.BlockSpec((1, 1, tn), lambda l, j: (l, 0, j))],
        out_specs=pl.BlockSpec((1, r, tn), lambda l, j: (l, 0, j)),
        out_shape=jax.ShapeDtypeStruct((n_layers, r, n), F32),
        compiler_params=_cparams(2),
        name="adaln",
    )(c_all, w_mod, b_mod.reshape(n_layers, 1, n))


def _inproj_kernel(x_ref, g_ref, m_ref, *rest, shift_row, n_w, swiglu):
    w_refs = rest[:n_w]
    o_ref = rest[n_w]
    h_ref = rest[n_w + 1]

    @pl.when(pl.program_id(2) == 0)
    def _():
        x = x_ref[0]
        ms = jnp.mean(x * x, axis=-1, keepdims=True)
        y = x * lax.rsqrt(ms + NORM_EPS) * g_ref[...]
        shift = m_ref[0, shift_row:shift_row + 1, :]
        scale = m_ref[0, shift_row + 1:shift_row + 2, :]
        h_ref[...] = (y * (1.0 + scale) + shift).astype(BF16)

    h = h_ref[...]
    if swiglu:
        gate = _dot(h, w_refs[0][...])
        up = _dot(h, w_refs[1][...])
        o_ref[0] = (_silu(gate) * up).astype(o_ref.dtype)
    else:
        o_ref[0] = _dot(h, w_refs[0][...]).astype(o_ref.dtype)


def inproj(x3, g_row, mods3, mod_row0, shift_row, w, n_out, tn, out_dtype, swiglu=False):
    bx, s, d = x3.shape
    tm = min(TM_IN, s)
    offs = (0, n_out // tn) if swiglu else (0,)
    in_specs = [pl.BlockSpec((1, tm, d), lambda b, i, j: (b, i, 0)),
                pl.BlockSpec((1, d), lambda b, i, j: (0, 0)),
                pl.BlockSpec((1, N_MOD, d), lambda b, i, j: (b + mod_row0, 0, 0))]
    for off in offs:
        in_specs.append(pl.BlockSpec((d, tn), lambda b, i, j, off=off: (0, j + off)))
    return pl.pallas_call(
        functools.partial(_inproj_kernel, shift_row=shift_row, n_w=len(offs), swiglu=swiglu),
        grid=(bx, s // tm, n_out // tn),
        in_specs=in_specs,
        out_specs=pl.BlockSpec((1, tm, tn), lambda b, i, j: (b, i, j)),
        out_shape=jax.ShapeDtypeStruct((bx, s, n_out), out_dtype),
        scratch_shapes=[pltpu.VMEM((tm, d), BF16)],
        compiler_params=_cparams(3),
        name="inproj_swiglu" if swiglu else "inproj",
    )(x3, g_row, mods3, *([w] * len(offs)))


def _outproj_kernel(*refs, mode, gate_row, nk):
    n_in = {"plain": 1, "hgrn": 4, "ssd": 6}[mode]
    ins = refs[:n_in]
    w_ref, res_ref, g_ref, m_ref, o_ref, acc_ref = refs[n_in:]
    k = pl.program_id(2)

    if mode == "plain":
        a = ins[0][0]
    elif mode == "hgrn":
        of_ref, ob_ref, gp_ref, og_ref = ins
        o = of_ref[0] + ob_ref[0]
        pieces = []
        for h in range(o.shape[-1] // HEAD_128):
            sl = slice(h * HEAD_128, (h + 1) * HEAD_128)
            oh = o[:, sl]
            ms = jnp.mean(oh * oh, axis=-1, keepdims=True)
            pieces.append(oh * lax.rsqrt(ms + NORM_EPS) * og_ref[:, sl])
        a = (jnp.concatenate(pieces, axis=-1) * _silu(gp_ref[0])).astype(BF16)
    else:
        yf_ref, yb_ref, xs_ref, z_ref, dsk_ref, ng_ref = ins
        y = dsk_ref[...] * xs_ref[0] + yf_ref[0] + yb_ref[0]
        y = y * _silu(z_ref[0])
        ms = jnp.mean(y * y, axis=-1, keepdims=True)
        a = (y * lax.rsqrt(ms + NORM_EPS) * ng_ref[...]).astype(BF16)

    @pl.when(k == 0)
    def _():
        acc_ref[...] = jnp.zeros_like(acc_ref)

    acc_ref[...] += _dot(a, w_ref[...])

    @pl.when(k == nk - 1)
    def _():
        y = acc_ref[...]
        ms = jnp.mean(y * y, axis=-1, keepdims=True)
        n = y * lax.rsqrt(ms + NORM_EPS) * g_ref[...]
        o_ref[0] = res_ref[0] + m_ref[0, gate_row:gate_row + 1, :] * n


def outproj(mode, ins, in_col_blocks, row_ins, w, res3, g_row, mods3, mod_row0, gate_row, tk):
    bx, s, d = res3.shape
    kdim = w.shape[0]
    tm = min(TM_OUT, s)
    nk = kdim // tk
    in_specs = [pl.BlockSpec((1, tm, tk), lambda b, i, k, off=off: (b, i, k + off)) for off in in_col_blocks]
    in_specs += [pl.BlockSpec((1, tk), lambda b, i, k: (0, k)) for _ in row_ins]
    in_specs += [pl.BlockSpec((tk, d), lambda b, i, k: (k, 0)),
                 pl.BlockSpec((1, tm, d), lambda b, i, k: (b, i, 0)),
                 pl.BlockSpec((1, d), lambda b, i, k: (0, 0)),
                 pl.BlockSpec((1, N_MOD, d), lambda b, i, k: (b + mod_row0, 0, 0))]
    return pl.pallas_call(
        functools.partial(_outproj_kernel, mode=mode, gate_row=gate_row, nk=nk),
        grid=(bx, s // tm, nk),
        in_specs=in_specs,
        out_specs=pl.BlockSpec((1, tm, d), lambda b, i, k: (b, i, 0)),
        out_shape=jax.ShapeDtypeStruct((bx, s, d), F32),
        scratch_shapes=[pltpu.VMEM((tm, d), F32)],
        compiler_params=_cparams(3),
        name="outproj_" + mode,
    )(*ins, *row_ins, w, res3, g_row, mods3)


def _deinterleave_perm(width, block):
    idx = jnp.arange(width).reshape(width // block, block)
    return jnp.concatenate([idx[:, 0::2], idx[:, 1::2]], axis=1).reshape(width)


def _rope_tables(n, head_dim, reps):
    rows = n // GRID_W
    row = jnp.repeat(jnp.arange(rows, dtype=F32), GRID_W)
    col = jnp.tile(jnp.arange(GRID_W, dtype=F32), rows)
    axis_dim = head_dim // 2
    inv_freq = ROPE_THETA ** (-jnp.arange(0, axis_dim, 2, dtype=F32) / axis_dim)
    ang = jnp.concatenate([row[:, None] * inv_freq, col[:, None] * inv_freq], axis=-1)
    cos, sin = jnp.cos(ang), jnp.sin(ang)
    cos_t = jnp.tile(jnp.concatenate([cos, cos], axis=-1), (1, reps))
    sin_t = jnp.tile(jnp.concatenate([-sin, sin], axis=-1), (1, reps))
    return cos_t, sin_t


def _identity_tables(n):
    return jnp.ones((n, LANES), F32), jnp.zeros((n, LANES), F32)


def _rot_half_128(y):
    return pltpu.roll(y, 64, 1)


def _rot_half_64(y):
    lane = lax.broadcasted_iota(jnp.int32, y.shape, 1)
    first = jnp.bitwise_and(lane, 63) < 32
    return jnp.where(first, pltpu.roll(y, 96, 1), pltpu.roll(y, 32, 1))


def _gqa_norm_rope(x, gain, cos, sin):
    ms = jnp.mean(x * x, axis=-1, keepdims=True)
    y = x * lax.rsqrt(ms + NORM_EPS) * gain
    return y * cos + _rot_half_128(y) * sin


def _gqa_kernel(q_ref, cq_ref, sq_ref, qg_ref, kg_ref, *rest, n_seg, group, scale):
    seg_refs = rest[:4 * n_seg]
    o_ref = rest[4 * n_seg]
    scr = rest[4 * n_seg + 1:]

    @pl.when(pl.program_id(2) == 0)
    def _():
        for s in range(n_seg):
            k_ref, v_ref, ck_ref, sk_ref = seg_refs[4 * s:4 * s + 4]
            scr[2 * s][...] = _gqa_norm_rope(k_ref[0], kg_ref[...], ck_ref[...], sk_ref[...]).astype(BF16)
            scr[2 * s + 1][...] = v_ref[0].astype(BF16)

    for g in range(group):
        sl = slice(g * HEAD_128, (g + 1) * HEAD_128)
        qn = (_gqa_norm_rope(q_ref[0, :, sl], qg_ref[...], cq_ref[...], sq_ref[...]) * scale).astype(BF16)
        ss = [_dot_nt(qn, scr[2 * s][...]) for s in range(n_seg)]
        m = functools.reduce(jnp.maximum, [jnp.max(sc, axis=-1, keepdims=True) for sc in ss])
        ps = [jnp.exp(sc - m) for sc in ss]
        l = functools.reduce(jnp.add, [jnp.sum(p, axis=-1, keepdims=True) for p in ps])
        o = functools.reduce(jnp.add, [_dot(p.astype(BF16), scr[2 * s + 1][...]) for s, p in enumerate(ps)])
        o_ref[0, :, sl] = (o / l).astype(o_ref.dtype)


def gqa_attention(q_src, q_tabs, segs, q_gain, k_gain, n_kv):
    b, sq, _ = q_src.shape
    group = GQA_GROUP
    tq = min(TQ_GQA, sq)
    qw = group * HEAD_128
    k_blk0 = n_kv * group
    v_blk0 = k_blk0 + n_kv
    in_specs = [pl.BlockSpec((1, tq, qw), lambda bi, h, i: (bi, i, h)),
                pl.BlockSpec((tq, LANES), lambda bi, h, i: (i, 0)),
                pl.BlockSpec((tq, LANES), lambda bi, h, i: (i, 0)),
                pl.BlockSpec((1, LANES), lambda bi, h, i: (0, 0)),
                pl.BlockSpec((1, LANES), lambda bi, h, i: (0, 0))]
    args = [q_src, q_tabs[0], q_tabs[1], q_gain, k_gain]
    scratch = []
    for kv_src, (ck, sk) in segs:
        ks = kv_src.shape[1]
        in_specs += [pl.BlockSpec((1, ks, LANES), lambda bi, h, i: (bi, 0, k_blk0 + h)),
                     pl.BlockSpec((1, ks, LANES), lambda bi, h, i: (bi, 0, v_blk0 + h)),
                     pl.BlockSpec((ks, LANES), lambda bi, h, i: (0, 0)),
                     pl.BlockSpec((ks, LANES), lambda bi, h, i: (0, 0))]
        args += [kv_src, kv_src, ck, sk]
        scratch += [pltpu.VMEM((ks, LANES), BF16), pltpu.VMEM((ks, LANES), BF16)]
    return pl.pallas_call(
        functools.partial(_gqa_kernel, n_seg=len(segs), group=group, scale=HEAD_128 ** -0.5),
        grid=(b, n_kv, sq // tq),
        in_specs=in_specs,
        out_specs=pl.BlockSpec((1, tq, qw), lambda bi, h, i: (bi, i, h)),
        out_shape=jax.ShapeDtypeStruct((b, sq, n_kv * qw), BF16),
        scratch_shapes=scratch,
        compiler_params=_cparams(3),
        name="gqa_attention",
    )(*args)


def _diff_kernel(q_ref, cq_ref, sq_ref, lam_ref, sg_ref, *rest, n_seg, scale, lambda_init):
    seg_refs = rest[:4 * n_seg]
    o_ref = rest[4 * n_seg]
    scr = rest[4 * n_seg + 1:]

    @pl.when(pl.program_id(2) == 0)
    def _():
        for s in range(n_seg):
            k_ref, v_ref, ck_ref, sk_ref = seg_refs[4 * s:4 * s + 4]
            k = k_ref[0]
            scr[2 * s][...] = (k * ck_ref[...] + _rot_half_64(k) * sk_ref[...]).astype(BF16)
            scr[2 * s + 1][...] = v_ref[0].astype(BF16)

    q = q_ref[0]
    q = (q * cq_ref[...] + _rot_half_64(q) * sq_ref[...]) * scale
    lane = lax.broadcasted_iota(jnp.int32, q.shape, 1)
    zero = jnp.zeros_like(q)
    qs = [jnp.where(lane < 64, q, zero).astype(BF16), jnp.where(lane < 64, zero, q).astype(BF16)]

    lp = lam_ref[...]
    lam = (jnp.exp(jnp.sum(lp[0:1] * lp[1:2], axis=-1, keepdims=True))
           - jnp.exp(jnp.sum(lp[2:3] * lp[3:4], axis=-1, keepdims=True)) + lambda_init)

    probs, inv_l = [], []
    for c in range(2):
        ss = [_dot_nt(qs[c], scr[2 * s][...]) for s in range(n_seg)]
        m = functools.reduce(jnp.maximum, [jnp.max(sc, axis=-1, keepdims=True) for sc in ss])
        ps = [jnp.exp(sc - m) for sc in ss]
        l = functools.reduce(jnp.add, [jnp.sum(p, axis=-1, keepdims=True) for p in ps])
        probs.append(ps)
        inv_l.append(1.0 / l)
    w1 = inv_l[0]
    w2 = lam * inv_l[1]
    o = functools.reduce(jnp.add, [
        _dot((probs[0][s] * w1 - probs[1][s] * w2).astype(BF16), scr[2 * s + 1][...]) for s in range(n_seg)])
    ms = jnp.mean(o * o, axis=-1, keepdims=True)
    o_ref[0] = (o * lax.rsqrt(ms + NORM_EPS) * sg_ref[...] * (1.0 - lambda_init)).astype(o_ref.dtype)


def diff_attention(q_src, q_tabs, segs, lam_p, subln_g, lambda_init):
    b, sq, _ = q_src.shape
    nh = DIFF_HEADS
    tq = min(TQ_DIFF, sq)
    in_specs = [pl.BlockSpec((1, tq, LANES), lambda bi, h, i: (bi, i, h)),
                pl.BlockSpec((tq, LANES), lambda bi, h, i: (i, 0)),
                pl.BlockSpec((tq, LANES), lambda bi, h, i: (i, 0)),
                pl.BlockSpec(lam_p.shape, lambda bi, h, i: (0, 0)),
                pl.BlockSpec((1, LANES), lambda bi, h, i: (0, 0))]
    args = [q_src, q_tabs[0], q_tabs[1], lam_p, subln_g]
    scratch = []
    for kv_src, (ck, sk) in segs:
        ks = kv_src.shape[1]
        in_specs += [pl.BlockSpec((1, ks, LANES), lambda bi, h, i: (bi, 0, nh + h)),
                     pl.BlockSpec((1, ks, LANES), lambda bi, h, i: (bi, 0, 2 * nh + h)),
                     pl.BlockSpec((ks, LANES), lambda bi, h, i: (0, 0)),
                     pl.BlockSpec((ks, LANES), lambda bi, h, i: (0, 0))]
        args += [kv_src, kv_src, ck, sk]
        scratch += [pltpu.VMEM((ks, LANES), BF16), pltpu.VMEM((ks, LANES), BF16)]
    return pl.pallas_call(
        functools.partial(_diff_kernel, n_seg=len(segs), scale=(LANES // 2) ** -0.5, lambda_init=lambda_init),
        grid=(b, nh, sq // tq),
        in_specs=in_specs,
        out_specs=pl.BlockSpec((1, tq, LANES), lambda bi, h, i: (bi, i, h)),
        out_shape=jax.ShapeDtypeStruct((b, sq, nh * LANES), BF16),
        scratch_shapes=scratch,
        compiler_params=_cparams(3),
        name="diff_attention",
    )(*args)


def _hgrn_kernel(q_ref, f_ref, v_ref, lb_ref, s0_ref, o_ref, sout_ref, st_ref, *, reverse, hpb, blk_len, nblk):
    blk = pl.program_id(2)

    @pl.when(blk == 0)
    def _():
        st_ref[...] = s0_ref[0]

    n_chunks = blk_len // HGRN_CHUNK
    row = lax.broadcasted_iota(jnp.int32, (blk_len, blk_len), 0)
    col = lax.broadcasted_iota(jnp.int32, (blk_len, blk_len), 1)
    same = lax.shift_right_logical(row, 5) == lax.shift_right_logical(col, 5)
    tri = jnp.logical_and(same, (col >= row) if reverse else (col <= row))
    tri01 = tri.astype(F32).astype(BF16)
    same01 = same.astype(F32).astype(BF16)

    for h in range(hpb):
        sl = slice(h * HEAD_128, (h + 1) * HEAD_128)
        q = _silu(q_ref[0, :, sl])
        lb = lb_ref[0, :, sl]
        forget = lb + (1.0 - lb) * (1.0 / (1.0 + jnp.exp(-f_ref[0, :, sl])))
        k = 1.0 - forget
        lf = jnp.log(forget)
        cum = _dot_exact_lhs(tri01, lf)
        last = _dot_exact_lhs(same01, lf)
        q_dec = (q * jnp.exp(cum)).astype(BF16)
        k_inv = (k * jnp.exp(-cum)).astype(BF16)
        k_end = (k * jnp.exp(last - cum)).astype(BF16)
        vb = v_ref[0, :, sl].astype(BF16)
        att = jnp.where(tri, _dot_nt(q_dec, k_inv), 0.0).astype(BF16)
        o_intra = _dot(att, vb)
        e_last = jnp.exp(last)
        for ci in range(n_chunks):
            c = n_chunks - 1 - ci if reverse else ci
            rs = slice(c * HGRN_CHUNK, (c + 1) * HGRN_CHUNK)
            st = st_ref[h]
            o_ref[0, rs, sl] = o_intra[rs] + _dot_nt(q_dec[rs], st.astype(BF16))
            st_ref[h] = st * e_last[c * HGRN_CHUNK:c * HGRN_CHUNK + 1, :] + _dot_tn(vb[rs], k_end[rs])

    @pl.when(blk == nblk - 1)
    def _():
        sout_ref[0] = st_ref[...]


def hgrn_scan(proj, lb3, s0, direction):
    b, s, w = proj.shape
    d = w // HGRN_N_PROJ
    nh = d // HEAD_128
    hpb = HGRN_HEADS_PER_STEP
    hw = hpb * HEAD_128
    blk_len = min(HGRN_BLOCK, s)
    nblk = s // blk_len
    ncb = d // hw
    reverse = direction == 1

    def tok(j):
        return nblk - 1 - j if reverse else j

    in_specs = [pl.BlockSpec((1, blk_len, hw), lambda bi, h, j: (bi, tok(j), h)),
                pl.BlockSpec((1, blk_len, hw), lambda bi, h, j: (bi, tok(j), (1 + direction) * ncb + h)),
                pl.BlockSpec((1, blk_len, hw), lambda bi, h, j: (bi, tok(j), 3 * ncb + h)),
                pl.BlockSpec((1, 1, hw), lambda bi, h, j: (direction, 0, h)),
                pl.BlockSpec((1, hpb, HEAD_128, HEAD_128), lambda bi, h, j: (bi, h, 0, 0))]
    return pl.pallas_call(
        functools.partial(_hgrn_kernel, reverse=reverse, hpb=hpb, blk_len=blk_len, nblk=nblk),
        grid=(b, nh // hpb, nblk),
        in_specs=in_specs,
        out_specs=[pl.BlockSpec((1, blk_len, hw), lambda bi, h, j: (bi, tok(j), h)),
                   pl.BlockSpec((1, hpb, HEAD_128, HEAD_128), lambda bi, h, j: (bi, h, 0, 0))],
        out_shape=[jax.ShapeDtypeStruct((b, s, d), F32),
                   jax.ShapeDtypeStruct((b, nh, HEAD_128, HEAD_128), F32)],
        scratch_shapes=[pltpu.VMEM((hpb, HEAD_128, HEAD_128), F32)],
        compiler_params=_cparams(3),
        name="hgrn_scan",
    )(proj, proj, proj, lb3, s0)


def _conv_silu_kernel(x_ref, w_ref, b_ref, o_ref):
    u = x_ref[0]
    n = u.shape[0]
    row = lax.broadcasted_iota(jnp.int32, u.shape, 0)
    prev = jnp.where(row == 0, 0.0, pltpu.roll(u, 1, 0))
    nxt = jnp.where(row == n - 1, 0.0, pltpu.roll(u, n - 1, 0))
    y = prev * w_ref[0:1, :] + u * w_ref[1:2, :] + nxt * w_ref[2:3, :] + b_ref[...]
    o_ref[0] = _silu(y)


def conv_silu(proj, col_blk0, width, conv_w, conv_b, tc=512):
    b, s, _ = proj.shape
    return pl.pallas_call(
        _conv_silu_kernel,
        grid=(b, width // tc),
        in_specs=[pl.BlockSpec((1, s, tc), lambda bi, j: (bi, 0, col_blk0 + j)),
                  pl.BlockSpec((3, tc), lambda bi, j: (0, j)),
                  pl.BlockSpec((1, tc), lambda bi, j: (0, j))],
        out_specs=pl.BlockSpec((1, s, tc), lambda bi, j: (bi, 0, j)),
        out_shape=jax.ShapeDtypeStruct((b, s, width), F32),
        compiler_params=_cparams(2),
        name="ssd_conv_silu",
    )(proj, conv_w, conv_b)


def _ssd_kernel(x_ref, bc_ref, dt_ref, dtb_ref, alog_ref, h0_ref, y_ref, hout_ref, st_ref,
                *, reverse, direction, nck, n_groups, hpg):
    ck = pl.program_id(1)
    lc = SSD_CHUNK
    gw = hpg * SSD_HEAD_DIM
    n_heads = n_groups * hpg

    @pl.when(ck == 0)
    def _():
        st_ref[...] = h0_ref[0]

    row = lax.broadcasted_iota(jnp.int32, (lc, lc), 0)
    col = lax.broadcasted_iota(jnp.int32, (lc, lc), 1)
    tri = (col >= row) if reverse else (col <= row)
    tri01 = tri.astype(F32).astype(BF16)
    lo = lax.broadcasted_iota(jnp.int32, (lc, LANES), 1) < SSD_HEAD_DIM
    lo_row = lo[0:1, :]

    x_in = dt_ref[0] + dtb_ref[...]
    dt = jnp.maximum(x_in, 0.0) + jnp.log1p(jnp.exp(-jnp.abs(x_in)))
    a = -jnp.exp(alog_ref[...])
    cum = _dot_exact_lhs(tri01, dt * a)
    cum_t = cum.T
    dt_t = dt.T
    t_last = 0 if reverse else lc - 1

    for g in range(n_groups):
        b_g = bc_ref[0, :, g * SSD_STATE:(g + 1) * SSD_STATE]
        c_g = bc_ref[0, :, (n_groups + g) * SSD_STATE:(n_groups + g + 1) * SSD_STATE].astype(BF16)
        b_t = b_g.T
        cb = _dot_nt(c_g, b_g.astype(BF16))
        h_g = st_ref[g]
        y_state = _dot(c_g, h_g.astype(BF16))
        for p in range(hpg // 2):
            x2 = x_ref[0, :, g * gw + p * LANES:g * gw + (p + 1) * LANES]
            zero = jnp.zeros_like(x2)
            x_halves = [jnp.where(lo, x2, zero).astype(BF16), jnp.where(lo, zero, x2).astype(BF16)]
            y_parts, upd_parts, e_cols, decays = [], [], [], []
            for jj in range(2):
                r = direction * n_heads + g * hpg + 2 * p + jj
                row_b = jnp.broadcast_to(cum_t[r:r + 1, :], (lc, lc))
                col_b = row_b.T
                dt_row = dt_t[r:r + 1, :]
                decay = jnp.exp(jnp.where(tri, col_b - row_b, -jnp.inf))
                w = (cb * decay * dt_row).astype(BF16)
                y_parts.append(_dot(w, x_halves[jj]))
                last_b = col_b[t_last:t_last + 1, :]
                coef_row = jnp.exp(last_b - row_b[0:1, :]) * dt_row
                upd_parts.append(_dot((b_t * coef_row).astype(BF16), x_halves[jj]))
                e_cols.append(jnp.exp(col_b))
                decays.append(jnp.exp(last_b))
            cs = slice(p * LANES, (p + 1) * LANES)
            y_ref[0, :, g * gw + p * LANES:g * gw + (p + 1) * LANES] = (
                y_parts[0] + y_parts[1] + y_state[:, cs] * jnp.where(lo, e_cols[0], e_cols[1]))
            st_ref[g, :, cs] = (h_g[:, cs] * jnp.where(lo_row, decays[0], decays[1])
                                + upd_parts[0] + upd_parts[1])

    @pl.when(ck == nck - 1)
    def _():
        hout_ref[0] = st_ref[...]


def ssd_scan(xs, bc, proj, dt_blk, dt_bias, a_log, h0, direction):
    b, s, d_inner = xs.shape
    n_groups = SSD_GROUPS
    hpg = d_inner // SSD_HEAD_DIM // n_groups
    gw = hpg * SSD_HEAD_DIM
    lc = SSD_CHUNK
    nck = s // lc
    reverse = direction == 1

    def tok(j):
        return nck - 1 - j if reverse else j

    return pl.pallas_call(
        functools.partial(_ssd_kernel, reverse=reverse, direction=direction, nck=nck, n_groups=n_groups, hpg=hpg),
        grid=(b, nck),
        in_specs=[pl.BlockSpec((1, lc, d_inner), lambda bi, j: (bi, tok(j), 0)),
                  pl.BlockSpec((1, lc, bc.shape[2]), lambda bi, j: (bi, tok(j), 0)),
                  pl.BlockSpec((1, lc, LANES), lambda bi, j: (bi, tok(j), dt_blk)),
                  pl.BlockSpec((1, LANES), lambda bi, j: (0, 0)),
                  pl.BlockSpec((1, LANES), lambda bi, j: (0, 0)),
                  pl.BlockSpec((1, n_groups, SSD_STATE, gw), lambda bi, j: (bi, 0, 0, 0))],
        out_specs=[pl.BlockSpec((1, lc, d_inner), lambda bi, j: (bi, tok(j), 0)),
                   pl.BlockSpec((1, n_groups, SSD_STATE, gw), lambda bi, j: (bi, 0, 0, 0))],
        out_shape=[jax.ShapeDtypeStruct((b, s, d_inner), F32),
                   jax.ShapeDtypeStruct((b, n_groups, SSD_STATE, gw), F32)],
        scratch_shapes=[pltpu.VMEM((n_groups, SSD_STATE, gw), F32)],
        compiler_params=_cparams(2),
        name="ssd_scan",
    )(xs, bc, proj, dt_bias, a_log, h0)


def _mixer_gqa(proj_l, proj_c, w_out, q_g, k_g, need_ctx, out_args_l, out_args_c):
    n = proj_l.shape[1]
    c = proj_c.shape[1]
    n_kv = proj_l.shape[2] // HEAD_128 // (GQA_GROUP + 2)
    perm = _deinterleave_perm(HEAD_128, HEAD_128)
    q_gain = q_g[perm].reshape(1, HEAD_128)
    k_gain = k_g[perm].reshape(1, HEAD_128)
    lat_tabs = _rope_tables(n, HEAD_128, 1)
    ctx_tabs = _identity_tables(c)
    o_l = gqa_attention(proj_l, lat_tabs, [(proj_l, lat_tabs), (proj_c, ctx_tabs)], q_gain, k_gain, n_kv)
    x_l = outproj("plain", [o_l], [0], [], w_out, *out_args_l, tk=TK_OUT)
    x_c = None
    if need_ctx:
        o_c = gqa_attention(proj_c, ctx_tabs, [(proj_c, ctx_tabs)], q_gain, k_gain, n_kv)
        x_c = outproj("plain", [o_c.reshape(1, -1, o_c.shape[2])], [0], [], w_out, *out_args_c, tk=TK_OUT)
    return x_l, x_c


def _mixer_hgrn(proj_l, proj_c, w_out, lb, out_g, need_ctx, out_args_l, out_args_c):
    b, _, w = proj_l.shape
    d = w // HGRN_N_PROJ
    nh = d // HEAD_128
    lb3 = lb.reshape(2, 1, d)
    s0 = jnp.zeros((b, nh, HEAD_128, HEAD_128), F32)
    o_l, o_c = [], []
    for direction in range(2):
        oc, s_ctx = hgrn_scan(proj_c, lb3, s0, direction)
        ol, _ = hgrn_scan(proj_l, lb3, s_ctx, direction)
        o_l.append(ol)
        o_c.append(oc)
    og = jnp.tile(out_g, nh).reshape(1, d)
    g_blk0 = 4 * (d // TK_OUT)
    x_l = outproj("hgrn", [o_l[0], o_l[1], proj_l], [0, 0, g_blk0], [og], w_out, *out_args_l, tk=TK_OUT)
    x_c = None
    if need_ctx:
        flat = lambda t: t.reshape(1, -1, t.shape[2])
        x_c = outproj("hgrn", [flat(o_c[0]), flat(o_c[1]), flat(proj_c)], [0, 0, g_blk0], [og], w_out,
                      *out_args_c, tk=TK_OUT)
    return x_l, x_c


def _mixer_ssd(proj_l, proj_c, w_out, conv_w, conv_b, dt_bias, a_log, d_skip, norm_g, need_ctx,
               out_args_l, out_args_c):
    b = proj_l.shape[0]
    d_inner = w_out.shape[0]
    n_heads = d_inner // SSD_HEAD_DIM
    hpg = n_heads // SSD_GROUPS
    gn = SSD_GROUPS * SSD_STATE
    tc = 512
    x_blk0 = d_inner // tc
    dt_blk = (2 * d_inner + 2 * gn) // LANES
    cw_x, cw_bc = conv_w[:, :d_inner], conv_w[:, d_inner:]
    cb_x, cb_bc = conv_b[:d_inner].reshape(1, -1), conv_b[d_inner:].reshape(1, -1)
    dtb = dt_bias.reshape(1, 2 * n_heads)
    alog = a_log.reshape(1, 2 * n_heads)
    h0 = jnp.zeros((b, SSD_GROUPS, SSD_STATE, hpg * SSD_HEAD_DIM), F32)

    def prep(proj):
        xs = conv_silu(proj, x_blk0, d_inner, cw_x, cb_x, tc)
        bc = conv_silu(proj, x_blk0 + d_inner // tc, 2 * gn, cw_bc, cb_bc, tc)
        return xs, bc

    xs_l, bc_l = prep(proj_l)
    xs_c, bc_c = prep(proj_c)
    y_l, y_c = [], []
    for direction in range(2):
        yc, h_ctx = ssd_scan(xs_c, bc_c, proj_c, dt_blk, dtb, alog, h0, direction)
        yl, _ = ssd_scan(xs_l, bc_l, proj_l, dt_blk, dtb, alog, h_ctx, direction)
        y_l.append(yl)
        y_c.append(yc)
    dsk = jnp.repeat(d_skip, SSD_HEAD_DIM).reshape(1, d_inner)
    ng = norm_g.reshape(1, d_inner)
    tk = d_inner // SSD_GROUPS
    x_l = outproj("ssd", [y_l[0], y_l[1], xs_l, proj_l], [0, 0, 0, 0], [dsk, ng], w_out, *out_args_l, tk=tk)
    x_c = None
    if need_ctx:
        flat = lambda t: t.reshape(1, -1, t.shape[2])
        x_c = outproj("ssd", [flat(y_c[0]), flat(y_c[1]), flat(xs_c), flat(proj_c)], [0, 0, 0, 0], [dsk, ng],
                      w_out, *out_args_c, tk=tk)
    return x_l, x_c


def _mixer_diff(proj_l, proj_c, w_out, lam_p, subln_g, lambda_init, out_args_l):
    n = proj_l.shape[1]
    c = proj_c.shape[1]
    lat_tabs = _rope_tables(n, LANES // 2, 2)
    ctx_tabs = _identity_tables(c)
    o_l = diff_attention(proj_l, lat_tabs, [(proj_l, lat_tabs), (proj_c, ctx_tabs)], lam_p,
                         subln_g.reshape(1, LANES), lambda_init)
    return outproj("plain", [o_l], [0], [], w_out, *out_args_l, tk=TK_OUT)


def kernel(x, c, ctx, c_ctx, w_mod, b_mod, norm_g, ffn_w13, ffn_w2, attn_w_in, attn_q_g, attn_k_g, attn_w_out, hgrn_w_in, hgrn_lb_logits, hgrn_out_g, hgrn_w_out, ssd_w_in, ssd_conv_w, ssd_conv_b, ssd_dt_bias, ssd_a_log, ssd_d, ssd_norm_g, ssd_w_out, diff_w_in, diff_lambda, diff_subln_g, diff_w_out):
    b, n, d = x.shape
    n_ctx = ctx.shape[1]
    depth = w_mod.shape[0]
    d_ff = ffn_w2.shape[1]
    n_mixers = 4

    n_rows = -(-(b + 1) // 16) * 16
    c_all = jnp.zeros((n_rows, d), F32).at[:b].set(c).at[b].set(c_ctx)
    mods = adaln_all(c_all, w_mod, b_mod).reshape(depth, n_rows, N_MOD, d)

    x_lat = x
    x_ctx = ctx.reshape(1, b * n_ctx, d)
    for layer in range(depth):
        kind, j = layer % n_mixers, layer // n_mixers
        need_ctx = layer < depth - 1
        m3 = mods[layer]
        g = norm_g[layer]
        g_rows = [g[i].reshape(1, d) for i in range(4)]

        if kind == 0:
            perm = _deinterleave_perm(attn_w_in.shape[2], HEAD_128)
            n_qk = (attn_w_in.shape[2] // HEAD_128 // (GQA_GROUP + 2)) * (GQA_GROUP + 1) * HEAD_128
            perm = jnp.where(jnp.arange(perm.shape[0]) < n_qk, perm, jnp.arange(perm.shape[0]))
            w_in = attn_w_in[j][:, perm]
        elif kind == 1:
            w_in = hgrn_w_in[j]
        elif kind == 2:
            w_in = ssd_w_in[j]
        else:
            perm = _deinterleave_perm(diff_w_in.shape[2], LANES // 2)
            n_qk = 2 * (diff_w_in.shape[2] // 3)
            perm = jnp.where(jnp.arange(perm.shape[0]) < n_qk, perm, jnp.arange(perm.shape[0]))
            w_in = diff_w_in[j][:, perm]
        w_in = w_in.astype(BF16)
        n_proj = w_in.shape[1]
        tn = 1152 if n_proj % 512 else 512

        proj_l = inproj(x_lat, g_rows[0], m3, 0, 0, w_in, n_proj, tn, F32)
        proj_c = inproj(x_ctx, g_rows[0], m3, b, 0, w_in, n_proj, tn, F32).reshape(b, n_ctx, n_proj)
        out_args_l = (x_lat, g_rows[1], m3, 0, 2)
        out_args_c = (x_ctx, g_rows[1], m3, b, 2)

        if kind == 0:
            x_lat, x_ctx_new = _mixer_gqa(proj_l, proj_c, attn_w_out[j].astype(BF16), attn_q_g[j], attn_k_g[j],
                                          need_ctx, out_args_l, out_args_c)
        elif kind == 1:
            cum = jnp.cumsum(jax.nn.softmax(hgrn_lb_logits.astype(F32), axis=1), axis=1)
            lb = cum[:, layer] - cum[:, 0]
            x_lat, x_ctx_new = _mixer_hgrn(proj_l, proj_c, hgrn_w_out[j].astype(BF16), lb, hgrn_out_g[j],
                                           need_ctx, out_args_l, out_args_c)
        elif kind == 2:
            x_lat, x_ctx_new = _mixer_ssd(proj_l, proj_c, ssd_w_out[j].astype(BF16), ssd_conv_w[j], ssd_conv_b[j],
                                          ssd_dt_bias[j], ssd_a_log[j], ssd_d[j], ssd_norm_g[j], need_ctx,
                                          out_args_l, out_args_c)
        else:
            lambda_init = 0.8 - 0.6 * math.exp(-0.3 * layer)
            x_lat = _mixer_diff(proj_l, proj_c, diff_w_out[j].astype(BF16), diff_lambda[j], diff_subln_g[j],
                                lambda_init, out_args_l)
            x_ctx_new = None

        w13 = ffn_w13[layer].astype(BF16)
        w2 = ffn_w2[layer].astype(BF16)
        u_l = inproj(x_lat, g_rows[2], m3, 0, 3, w13, d_ff, 512, BF16, swiglu=True)
        x_lat = outproj("plain", [u_l], [0], [], w2, x_lat, g_rows[3], m3, 0, 5, tk=TK_OUT)
        if need_ctx:
            x_ctx = x_ctx_new
            u_c = inproj(x_ctx, g_rows[2], m3, b, 3, w13, d_ff, 512, BF16, swiglu=True)
            x_ctx = outproj("plain", [u_c], [0], [], w2, x_ctx, g_rows[3], m3, b, 5, tk=TK_OUT)
    return x_lat
```

```python
import functools
import math

import jax
import jax.numpy as jnp
from jax import lax
from jax.experimental import pallas as pl
from jax.experimental.pallas import tpu as pltpu

F32 = jnp.float32
BF16 = jnp.bfloat16

NORM_EPS = 1e-6
LOG2_E = math.log2(math.e)
ROPE_THETA = 10000.0
GRID_W = 64
N_MOD = 6
HEAD_128 = 128
GQA_GROUP = 4
HGRN_CHUNK = 32
HGRN_N_PROJ = 5
SSD_HEAD_DIM = 64
SSD_GROUPS = 8
SSD_STATE = 128
DIFF_HEADS = 16

LANES = 128
VMEM_LIMIT_BYTES = 56 * 2**20
TM_IN = 1024
TM_OUT = 512
ROW_CHUNK = 16
ROW_UNROLL = 8
TQ_GQA = 256
TQ_DIFF = 512
HGRN_BLOCK = 256
HGRN_HEADS_PER_STEP = 4
SSD_CHUNK = 128


def _cparams(n_axes):
    return pltpu.CompilerParams(dimension_semantics=("arbitrary",) * n_axes,
                                vmem_limit_bytes=VMEM_LIMIT_BYTES)


def _silu(x):
    return x / (1.0 + jnp.exp(-x))


def _dot(a, b):
    return jnp.dot(a, b, preferred_element_type=F32)


def _dot_nt(a, b):
    return lax.dot_general(a, b, (((1,), (1,)), ((), ())), preferred_element_type=F32)


def _dot_tn(a, b):
    return lax.dot_general(a, b, (((0,), (0,)), ((), ())), preferred_element_type=F32)


def _dot_exact_lhs(m01, x):
    hi = x.astype(BF16)
    r1 = x - hi.astype(F32)
    mid = r1.astype(BF16)
    lo = (r1 - mid.astype(F32)).astype(BF16)
    return _dot(m01, hi) + _dot(m01, mid) + _dot(m01, lo)


def _adaln_kernel(c_ref, w_ref, b_ref, o_ref):
    s = _silu(c_ref[...]).astype(BF16)
    o_ref[0] = _dot(s, w_ref[0].astype(BF16)) + b_ref[0]


def adaln_all(c_all, w_mod, b_mod):
    n_layers, d, n = w_mod.shape
    r = c_all.shape[0]
    tn = 1024
    return pl.pallas_call(
        _adaln_kernel,
        grid=(n_layers, n // tn),
        in_specs=[pl.BlockSpec((r, d), lambda l, j: (0, 0)),
                  pl.BlockSpec((1, d, tn), lambda l, j: (l, 0, j)),
                  pl.BlockSpec((1, 1, tn), lambda l, j: (l, 0, j))],
        out_specs=pl.BlockSpec((1, r, tn), lambda l, j: (l, 0, j)),
        out_shape=jax.ShapeDtypeStruct((n_layers, r, n), F32),
        compiler_params=_cparams(2),
        name="adaln",
    )(c_all, w_mod, b_mod.reshape(n_layers, 1, n))


def _inproj_kernel(x_ref, g_ref, m_ref, *rest, shift_row, n_w, swiglu):
    w_refs = rest[:n_w]
    o_ref = rest[n_w]
    h_ref = rest[n_w + 1]

    @pl.when(pl.program_id(2) == 0)
    def _():
        def rows(i, carry):
            r0 = pl.multiple_of(i * ROW_CHUNK, ROW_CHUNK)
            x = x_ref[0, pl.ds(r0, ROW_CHUNK), :]
            ms = jnp.mean(x * x, axis=-1, keepdims=True)
            y = x * lax.rsqrt(ms + NORM_EPS) * g_ref[...]
            shift = m_ref[0, shift_row:shift_row + 1, :]
            scale = m_ref[0, shift_row + 1:shift_row + 2, :]
            h_ref[pl.ds(r0, ROW_CHUNK), :] = (y * (1.0 + scale) + shift).astype(BF16)
            return carry

        lax.fori_loop(0, h_ref.shape[0] // ROW_CHUNK, rows, 0, unroll=ROW_UNROLL)

    h = h_ref[...]
    if swiglu:
        half = o_ref.shape[2] // 2
        for c in range(2):
            cs = slice(c * half, (c + 1) * half)
            gate = _dot(h, w_refs[0][:, cs])
            up = _dot(h, w_refs[1][:, cs])
            o_ref[0, :, cs] = (_silu(gate) * up).astype(o_ref.dtype)
    else:
        o_ref[0] = _dot(h, w_refs[0][...]).astype(o_ref.dtype)


def inproj(x3, g_row, mods3, mod_row0, shift_row, w, n_out, tn, out_dtype, swiglu=False):
    bx, s, d = x3.shape
    tm = min(TM_IN, s)
    offs = (0, n_out // tn) if swiglu else (0,)
    in_specs = [pl.BlockSpec((1, tm, d), lambda b, i, j: (b, i, 0)),
                pl.BlockSpec((1, d), lambda b, i, j: (0, 0)),
                pl.BlockSpec((1, N_MOD, d), lambda b, i, j: (b + mod_row0, 0, 0))]
    for off in offs:
        in_specs.append(pl.BlockSpec((d, tn), lambda b, i, j, off=off: (0, j + off)))
    return pl.pallas_call(
        functools.partial(_inproj_kernel, shift_row=shift_row, n_w=len(offs), swiglu=swiglu),
        grid=(bx, s // tm, n_out // tn),
        in_specs=in_specs,
        out_specs=pl.BlockSpec((1, tm, tn), lambda b, i, j: (b, i, j)),
        out_shape=jax.ShapeDtypeStruct((bx, s, n_out), out_dtype),
        scratch_shapes=[pltpu.VMEM((tm, d), BF16)],
        compiler_params=_cparams(3),
        name="inproj_swiglu" if swiglu else "inproj",
    )(x3, g_row, mods3, *([w] * len(offs)))


def _outproj_kernel(*refs, mode, gate_row, nk, norm_group):
    n_in = {"plain": 1, "hgrn": 4, "ssd": 6}[mode]
    ins = refs[:n_in]
    w_ref, res_ref, g_ref, m_ref, o_ref, acc_ref = refs[n_in:]
    k = pl.program_id(2)

    if mode == "plain":
        a = ins[0][0]
    elif mode == "hgrn":
        of_ref, ob_ref, gp_ref, og_ref = ins
        o = of_ref[0] + ob_ref[0]
        pieces = []
        for h in range(o.shape[-1] // HEAD_128):
            sl = slice(h * HEAD_128, (h + 1) * HEAD_128)
            oh = o[:, sl]
            ms = jnp.mean(oh * oh, axis=-1, keepdims=True)
            pieces.append(oh * lax.rsqrt(ms + NORM_EPS) * og_ref[:, sl])
        a = (jnp.concatenate(pieces, axis=-1) * _silu(gp_ref[0])).astype(BF16)
    else:
        yf_ref, yb_ref, xs_ref, z_ref, dsk_ref, ng_ref = ins
        pieces = []
        for gi in range(xs_ref.shape[2] // norm_group):
            sl = slice(gi * norm_group, (gi + 1) * norm_group)
            y = dsk_ref[:, sl] * xs_ref[0, :, sl] + yf_ref[0, :, sl] + yb_ref[0, :, sl]
            y = y * _silu(z_ref[0, :, sl])
            ms = jnp.mean(y * y, axis=-1, keepdims=True)
            pieces.append((y * lax.rsqrt(ms + NORM_EPS) * ng_ref[:, sl]).astype(BF16))
        a = pieces[0] if len(pieces) == 1 else jnp.concatenate(pieces, axis=-1)

    if nk == 1:
        acc_ref[...] = _dot(a, w_ref[...])
    else:
        @pl.when(k == 0)
        def _():
            acc_ref[...] = jnp.zeros_like(acc_ref)

        acc_ref[...] += _dot(a, w_ref[...])

    @pl.when(k == nk - 1)
    def _():
        def rows(i, carry):
            r0 = pl.multiple_of(i * ROW_CHUNK, ROW_CHUNK)
            y = acc_ref[pl.ds(r0, ROW_CHUNK), :]
            ms = jnp.mean(y * y, axis=-1, keepdims=True)
            n = y * lax.rsqrt(ms + NORM_EPS) * g_ref[...]
            o_ref[0, pl.ds(r0, ROW_CHUNK), :] = (res_ref[0, pl.ds(r0, ROW_CHUNK), :]
                                                 + m_ref[0, gate_row:gate_row + 1, :] * n)
            return carry

        lax.fori_loop(0, acc_ref.shape[0] // ROW_CHUNK, rows, 0, unroll=ROW_UNROLL)


def _outproj_tk(kdim, cap):
    for nk in range(1, kdim // LANES + 1):
        if kdim % nk == 0 and (kdim // nk) % LANES == 0 and kdim // nk <= cap:
            return kdim // nk
    return LANES


def outproj(mode, ins, in_cols, row_ins, w, res3, g_row, mods3, mod_row0, gate_row, tk=None, norm_group=0):
    bx, s, d = res3.shape
    kdim = w.shape[0]
    tm = min(TM_OUT, s)
    tk = tk or _outproj_tk(kdim, 2048 if mode == "plain" else 1024)
    nk = kdim // tk
    in_col_blocks = [c // tk for c in in_cols]
    in_specs = [pl.BlockSpec((1, tm, tk), lambda b, i, k, off=off: (b, i, k + off)) for off in in_col_blocks]
    in_specs += [pl.BlockSpec((1, tk), lambda b, i, k: (0, k)) for _ in row_ins]
    in_specs += [pl.BlockSpec((tk, d), lambda b, i, k: (k, 0)),
                 pl.BlockSpec((1, tm, d), lambda b, i, k: (b, i, 0)),
                 pl.BlockSpec((1, d), lambda b, i, k: (0, 0)),
                 pl.BlockSpec((1, N_MOD, d), lambda b, i, k: (b + mod_row0, 0, 0))]
    return pl.pallas_call(
        functools.partial(_outproj_kernel, mode=mode, gate_row=gate_row, nk=nk, norm_group=norm_group),
        grid=(bx, s // tm, nk),
        in_specs=in_specs,
        out_specs=pl.BlockSpec((1, tm, d), lambda b, i, k: (b, i, 0)),
        out_shape=jax.ShapeDtypeStruct((bx, s, d), F32),
        scratch_shapes=[pltpu.VMEM((tm, d), F32)],
        compiler_params=_cparams(3),
        name="outproj_" + mode,
    )(*ins, *row_ins, w, res3, g_row, mods3)


def _deinterleave_perm(width, block):
    idx = jnp.arange(width).reshape(width // block, block)
    return jnp.concatenate([idx[:, 0::2], idx[:, 1::2]], axis=1).reshape(width)


def _rope_tables(n, head_dim, reps):
    rows = n // GRID_W
    row = jnp.repeat(jnp.arange(rows, dtype=F32), GRID_W)
    col = jnp.tile(jnp.arange(GRID_W, dtype=F32), rows)
    axis_dim = head_dim // 2
    inv_freq = ROPE_THETA ** (-jnp.arange(0, axis_dim, 2, dtype=F32) / axis_dim)
    ang = jnp.concatenate([row[:, None] * inv_freq, col[:, None] * inv_freq], axis=-1)
    cos, sin = jnp.cos(ang), jnp.sin(ang)
    cos_t = jnp.tile(jnp.concatenate([cos, cos], axis=-1), (1, reps))
    sin_t = jnp.tile(jnp.concatenate([-sin, sin], axis=-1), (1, reps))
    return cos_t, sin_t


def _identity_tables(n):
    return jnp.ones((n, LANES), F32), jnp.zeros((n, LANES), F32)


def _rot_half_128(y):
    return pltpu.roll(y, 64, 1)


def _rot_half_64(y):
    lane = lax.broadcasted_iota(jnp.int32, y.shape, 1)
    first = jnp.bitwise_and(lane, 63) < 32
    return jnp.where(first, pltpu.roll(y, 96, 1), pltpu.roll(y, 32, 1))


def _col_reduce(x, op):
    n, t = x.shape
    if n > LANES:
        x = op(x.reshape(n // LANES, LANES, t), axis=0)
    return op(x, axis=0, keepdims=True)


def _softmax_numerators_t(scores):
    m = functools.reduce(jnp.maximum, [_col_reduce(sc, jnp.max) for sc in scores])
    ps = [jnp.exp2(sc - m) for sc in scores]
    l = functools.reduce(jnp.add, [_col_reduce(p, jnp.sum) for p in ps])
    return ps, l


def _softmax_pv_t(scores, v_t):
    ps, l = _softmax_numerators_t(scores)
    o_t = functools.reduce(jnp.add, [_dot(v, p.astype(BF16)) for v, p in zip(v_t, ps)])
    return o_t, l


def _gqa_norm_rope(x, gain, cos, sin):
    ms = jnp.mean(x * x, axis=-1, keepdims=True)
    y = x * lax.rsqrt(ms + NORM_EPS) * gain
    return y * cos + _rot_half_128(y) * sin


def _gqa_kernel(q_ref, cq_ref, sq_ref, qg_ref, kg_ref, *rest, n_seg, group, scale):
    seg_refs = rest[:4 * n_seg]
    o_ref = rest[4 * n_seg]
    scr = rest[4 * n_seg + 1:]

    @pl.when(pl.program_id(2) == 0)
    def _():
        for s in range(n_seg):
            k_ref, v_ref, ck_ref, sk_ref = seg_refs[4 * s:4 * s + 4]
            scr[2 * s][...] = _gqa_norm_rope(k_ref[0], kg_ref[...], ck_ref[...], sk_ref[...]).astype(BF16)
            scr[2 * s + 1][...] = v_ref[0].T.astype(BF16)

    heads = [slice(g * HEAD_128, (g + 1) * HEAD_128) for g in range(group)]
    qn = [(_gqa_norm_rope(q_ref[0, :, sl], qg_ref[...], cq_ref[...], sq_ref[...]) * (scale * LOG2_E)).astype(BF16)
          for sl in heads]
    def qk(g):
        return [_dot_nt(scr[2 * s][...], qn[g]) for s in range(n_seg)]

    scores = qk(0)
    for g, sl in enumerate(heads):
        nxt = qk(g + 1) if g + 1 < group else None
        o_t, l = _softmax_pv_t(scores, [scr[2 * s + 1][...] for s in range(n_seg)])
        o_ref[0, :, sl] = (o_t * (1.0 / l)).T.astype(o_ref.dtype)
        scores = nxt


def gqa_attention(q_src, q_tabs, segs, q_gain, k_gain, n_kv):
    b, sq, _ = q_src.shape
    group = GQA_GROUP
    tq = min(TQ_GQA, sq)
    qw = group * HEAD_128
    k_blk0 = n_kv * group
    v_blk0 = k_blk0 + n_kv
    in_specs = [pl.BlockSpec((1, tq, qw), lambda bi, h, i: (bi, i, h)),
                pl.BlockSpec((tq, LANES), lambda bi, h, i: (i, 0)),
                pl.BlockSpec((tq, LANES), lambda bi, h, i: (i, 0)),
                pl.BlockSpec((1, LANES), lambda bi, h, i: (0, 0)),
                pl.BlockSpec((1, LANES), lambda bi, h, i: (0, 0))]
    args = [q_src, q_tabs[0], q_tabs[1], q_gain, k_gain]
    scratch = []
    for kv_src, (ck, sk) in segs:
        ks = kv_src.shape[1]
        in_specs += [pl.BlockSpec((1, ks, LANES), lambda bi, h, i: (bi, 0, k_blk0 + h)),
                     pl.BlockSpec((1, ks, LANES), lambda bi, h, i: (bi, 0, v_blk0 + h)),
                     pl.BlockSpec((ks, LANES), lambda bi, h, i: (0, 0)),
                     pl.BlockSpec((ks, LANES), lambda bi, h, i: (0, 0))]
        args += [kv_src, kv_src, ck, sk]
        scratch += [pltpu.VMEM((ks, LANES), BF16), pltpu.VMEM((LANES, ks), BF16)]
    return pl.pallas_call(
        functools.partial(_gqa_kernel, n_seg=len(segs), group=group, scale=HEAD_128 ** -0.5),
        grid=(b, n_kv, sq // tq),
        in_specs=in_specs,
        out_specs=pl.BlockSpec((1, tq, qw), lambda bi, h, i: (bi, i, h)),
        out_shape=jax.ShapeDtypeStruct((b, sq, n_kv * qw), BF16),
        scratch_shapes=scratch,
        compiler_params=_cparams(3),
        name="gqa_attention",
    )(*args)


def _diff_kernel(q_ref, cq_ref, sq_ref, lam_ref, sg_ref, *rest, n_seg, scale, lambda_init):
    seg_refs = rest[:4 * n_seg]
    o_ref = rest[4 * n_seg]
    scr = rest[4 * n_seg + 1:]

    @pl.when(pl.program_id(2) == 0)
    def _():
        for s in range(n_seg):
            k_ref, v_ref, ck_ref, sk_ref = seg_refs[4 * s:4 * s + 4]
            k = k_ref[0]
            scr[2 * s][...] = (k * ck_ref[...] + _rot_half_64(k) * sk_ref[...]).astype(BF16)
            scr[2 * s + 1][...] = v_ref[0].T.astype(BF16)

    q = q_ref[0]
    q = (q * cq_ref[...] + _rot_half_64(q) * sq_ref[...]) * (scale * LOG2_E)
    lane = lax.broadcasted_iota(jnp.int32, q.shape, 1)
    zero = jnp.zeros_like(q)
    qs = [jnp.where(lane < 64, q, zero).astype(BF16), jnp.where(lane < 64, zero, q).astype(BF16)]

    lp = lam_ref[...]
    lam = (jnp.exp(jnp.sum(lp[0:1] * lp[1:2], axis=-1, keepdims=True))
           - jnp.exp(jnp.sum(lp[2:3] * lp[3:4], axis=-1, keepdims=True)) + lambda_init)

    scores = [[_dot_nt(scr[2 * s][...], qc) for s in range(n_seg)] for qc in qs]
    (p1, l1), (p2, l2) = [_softmax_numerators_t(sc) for sc in scores]
    r = lam * l1 / l2
    o_t = functools.reduce(jnp.add, [
        _dot(scr[2 * s + 1][...], (p1[s] - r * p2[s]).astype(BF16)) for s in range(n_seg)])
    o = (o_t * (1.0 / l1)).T
    ms = jnp.mean(o * o, axis=-1, keepdims=True)
    o_ref[0] = (o * lax.rsqrt(ms + NORM_EPS) * sg_ref[...] * (1.0 - lambda_init)).astype(o_ref.dtype)


def diff_attention(q_src, q_tabs, segs, lam_p, subln_g, lambda_init):
    b, sq, _ = q_src.shape
    nh = DIFF_HEADS
    tq = min(TQ_DIFF, sq)
    in_specs = [pl.BlockSpec((1, tq, LANES), lambda bi, h, i: (bi, i, h)),
                pl.BlockSpec((tq, LANES), lambda bi, h, i: (i, 0)),
                pl.BlockSpec((tq, LANES), lambda bi, h, i: (i, 0)),
                pl.BlockSpec(lam_p.shape, lambda bi, h, i: (0, 0)),
                pl.BlockSpec((1, LANES), lambda bi, h, i: (0, 0))]
    args = [q_src, q_tabs[0], q_tabs[1], lam_p, subln_g]
    scratch = []
    for kv_src, (ck, sk) in segs:
        ks = kv_src.shape[1]
        in_specs += [pl.BlockSpec((1, ks, LANES), lambda bi, h, i: (bi, 0, nh + h)),
                     pl.BlockSpec((1, ks, LANES), lambda bi, h, i: (bi, 0, 2 * nh + h)),
                     pl.BlockSpec((ks, LANES), lambda bi, h, i: (0, 0)),
                     pl.BlockSpec((ks, LANES), lambda bi, h, i: (0, 0))]
        args += [kv_src, kv_src, ck, sk]
        scratch += [pltpu.VMEM((ks, LANES), BF16), pltpu.VMEM((LANES, ks), BF16)]
    return pl.pallas_call(
        functools.partial(_diff_kernel, n_seg=len(segs), scale=(LANES // 2) ** -0.5, lambda_init=lambda_init),
        grid=(b, nh, sq // tq),
        in_specs=in_specs,
        out_specs=pl.BlockSpec((1, tq, LANES), lambda bi, h, i: (bi, i, h)),
        out_shape=jax.ShapeDtypeStruct((b, sq, nh * LANES), BF16),
        scratch_shapes=scratch,
        compiler_params=_cparams(3),
        name="diff_attention",
    )(*args)


def _hgrn_kernel(q_ref, f_ref, v_ref, lb_ref, s0_ref, o_ref, sout_ref, st_ref, *, reverse, hpb, blk_len, nblk):
    blk = pl.program_id(2)

    @pl.when(blk == 0)
    def _():
        st_ref[...] = s0_ref[0]

    n_chunks = blk_len // HGRN_CHUNK
    row = lax.broadcasted_iota(jnp.int32, (blk_len, blk_len), 0)
    col = lax.broadcasted_iota(jnp.int32, (blk_len, blk_len), 1)
    same = lax.shift_right_logical(row, 5) == lax.shift_right_logical(col, 5)
    tri = jnp.logical_and(same, (col >= row) if reverse else (col <= row))
    tri01 = tri.astype(F32).astype(BF16)
    heads = [slice(h * HEAD_128, (h + 1) * HEAD_128) for h in range(hpb)]
    chunks = [slice(c * HGRN_CHUNK, (c + 1) * HGRN_CHUNK) for c in range(n_chunks)]

    q = _silu(q_ref[0])
    lb = lb_ref[0]
    forget = lb + (1.0 - lb) * (1.0 / (1.0 + jnp.exp(-f_ref[0])))
    k = 1.0 - forget
    cum = _dot_exact_lhs(tri01, jnp.log(forget))
    last_rows = [cum[c.start:c.start + 1] if reverse else cum[c.stop - 1:c.stop] for c in chunks]
    last = jnp.concatenate([jnp.broadcast_to(r, (HGRN_CHUNK, r.shape[1])) for r in last_rows], axis=0)
    q_dec = (q * jnp.exp(cum)).astype(BF16)
    k_inv = (k * jnp.exp(-cum)).astype(BF16)
    k_end = (k * jnp.exp(last - cum)).astype(BF16)
    vb = v_ref[0].astype(BF16)
    e_last = [jnp.exp(r) for r in last_rows]
    att = [jnp.where(tri, _dot_nt(q_dec[:, sl], k_inv[:, sl]), 0.0).astype(BF16) for sl in heads]
    o_intra = [_dot(att[h], vb[:, sl]) for h, sl in enumerate(heads)]
    upd = [[_dot_tn(vb[rs, sl], k_end[rs, sl]) for sl in heads] for rs in chunks]
    st = [st_ref[h] for h in range(hpb)]
    for ci in range(n_chunks):
        c = n_chunks - 1 - ci if reverse else ci
        rs = chunks[c]
        for h, sl in enumerate(heads):
            o_ref[0, rs, sl] = o_intra[h][rs] + _dot_nt(q_dec[rs, sl], st[h].astype(BF16))
            st[h] = st[h] * e_last[c][:, sl] + upd[c][h]
    for h in range(hpb):
        st_ref[h] = st[h]

    @pl.when(blk == nblk - 1)
    def _():
        sout_ref[0] = st_ref[...]


def hgrn_scan(proj, lb3, s0, direction):
    b, s, w = proj.shape
    d = w // HGRN_N_PROJ
    nh = d // HEAD_128
    hpb = HGRN_HEADS_PER_STEP
    hw = hpb * HEAD_128
    blk_len = min(HGRN_BLOCK, s)
    nblk = s // blk_len
    ncb = d // hw
    reverse = direction == 1

    def tok(j):
        return nblk - 1 - j if reverse else j

    in_specs = [pl.BlockSpec((1, blk_len, hw), lambda bi, h, j: (bi, tok(j), h)),
                pl.BlockSpec((1, blk_len, hw), lambda bi, h, j: (bi, tok(j), (1 + direction) * ncb + h)),
                pl.BlockSpec((1, blk_len, hw), lambda bi, h, j: (bi, tok(j), 3 * ncb + h)),
                pl.BlockSpec((1, 1, hw), lambda bi, h, j: (direction, 0, h)),
                pl.BlockSpec((1, hpb, HEAD_128, HEAD_128), lambda bi, h, j: (bi, h, 0, 0))]
    return pl.pallas_call(
        functools.partial(_hgrn_kernel, reverse=reverse, hpb=hpb, blk_len=blk_len, nblk=nblk),
        grid=(b, nh // hpb, nblk),
        in_specs=in_specs,
        out_specs=[pl.BlockSpec((1, blk_len, hw), lambda bi, h, j: (bi, tok(j), h)),
                   pl.BlockSpec((1, hpb, HEAD_128, HEAD_128), lambda bi, h, j: (bi, h, 0, 0))],
        out_shape=[jax.ShapeDtypeStruct((b, s, d), F32),
                   jax.ShapeDtypeStruct((b, nh, HEAD_128, HEAD_128), F32)],
        scratch_shapes=[pltpu.VMEM((hpb, HEAD_128, HEAD_128), F32)],
        compiler_params=_cparams(3),
        name="hgrn_scan",
    )(proj, proj, proj, lb3, s0)


def _conv_silu_kernel(x_ref, w_ref, b_ref, o_ref):
    u = x_ref[0]
    n = u.shape[0]
    row = lax.broadcasted_iota(jnp.int32, u.shape, 0)
    prev = jnp.where(row == 0, 0.0, pltpu.roll(u, 1, 0))
    nxt = jnp.where(row == n - 1, 0.0, pltpu.roll(u, n - 1, 0))
    y = prev * w_ref[0:1, :] + u * w_ref[1:2, :] + nxt * w_ref[2:3, :] + b_ref[...]
    o_ref[0] = _silu(y)


def conv_silu(proj, col_blk0, width, conv_w, conv_b, tc=512):
    b, s, _ = proj.shape
    return pl.pallas_call(
        _conv_silu_kernel,
        grid=(b, width // tc),
        in_specs=[pl.BlockSpec((1, s, tc), lambda bi, j: (bi, 0, col_blk0 + j)),
                  pl.BlockSpec((3, tc), lambda bi, j: (0, j)),
                  pl.BlockSpec((1, tc), lambda bi, j: (0, j))],
        out_specs=pl.BlockSpec((1, s, tc), lambda bi, j: (bi, 0, j)),
        out_shape=jax.ShapeDtypeStruct((b, s, width), F32),
        compiler_params=_cparams(2),
        name="ssd_conv_silu",
    )(proj, conv_w, conv_b)


def _ssd_kernel(x_ref, bc_ref, dt_ref, dtb_ref, alog_ref, h0_ref, y_ref, hout_ref, st_ref,
                *, reverse, direction, nck, n_groups, hpg):
    ck = pl.program_id(1)
    lc = SSD_CHUNK
    gw = hpg * SSD_HEAD_DIM
    n_heads = n_groups * hpg

    @pl.when(ck == 0)
    def _():
        st_ref[...] = h0_ref[0]

    row = lax.broadcasted_iota(jnp.int32, (lc, lc), 0)
    col = lax.broadcasted_iota(jnp.int32, (lc, lc), 1)
    tri = (col >= row) if reverse else (col <= row)
    tri01 = tri.astype(F32).astype(BF16)
    lo = lax.broadcasted_iota(jnp.int32, (lc, LANES), 1) < SSD_HEAD_DIM
    lo_row = lo[0:1, :]

    x_in = dt_ref[0] + dtb_ref[...]
    dt = jnp.maximum(x_in, 0.0) + jnp.log1p(jnp.exp(-jnp.abs(x_in)))
    a = -jnp.exp(alog_ref[...])
    cum = _dot_exact_lhs(tri01, dt * a)
    cum_t = cum.T
    dt_t = dt.T
    t_last = 0 if reverse else lc - 1

    for g in range(n_groups):
        b_g = bc_ref[0, :, g * SSD_STATE:(g + 1) * SSD_STATE]
        c_g = bc_ref[0, :, (n_groups + g) * SSD_STATE:(n_groups + g + 1) * SSD_STATE].astype(BF16)
        b_t = b_g.T
        cb = _dot_nt(c_g, b_g.astype(BF16))
        h_g = st_ref[g]
        y_state = _dot(c_g, h_g.astype(BF16))
        for p in range(hpg // 2):
            x2 = x_ref[0, :, g * gw + p * LANES:g * gw + (p + 1) * LANES]
            zero = jnp.zeros_like(x2)
            x_halves = [jnp.where(lo, x2, zero).astype(BF16), jnp.where(lo, zero, x2).astype(BF16)]
            y_parts, upd_parts, e_cols, decays = [], [], [], []
            for jj in range(2):
                r = direction * n_heads + g * hpg + 2 * p + jj
                row_b = jnp.broadcast_to(cum_t[r:r + 1, :], (lc, lc))
                col_b = row_b.T
                dt_row = dt_t[r:r + 1, :]
                decay = jnp.exp(jnp.where(tri, col_b - row_b, -jnp.inf))
                w = (cb * decay * dt_row).astype(BF16)
                y_parts.append(_dot(w, x_halves[jj]))
                last_b = col_b[t_last:t_last + 1, :]
                coef_row = jnp.exp(last_b - row_b[0:1, :]) * dt_row
                upd_parts.append(_dot((b_t * coef_row).astype(BF16), x_halves[jj]))
                e_cols.append(jnp.exp(col_b))
                decays.append(jnp.exp(last_b))
            cs = slice(p * LANES, (p + 1) * LANES)
            y_ref[0, :, g * gw + p * LANES:g * gw + (p + 1) * LANES] = (
                y_parts[0] + y_parts[1] + y_state[:, cs] * jnp.where(lo, e_cols[0], e_cols[1]))
            st_ref[g, :, cs] = (h_g[:, cs] * jnp.where(lo_row, decays[0], decays[1])
                                + upd_parts[0] + upd_parts[1])

    @pl.when(ck == nck - 1)
    def _():
        hout_ref[0] = st_ref[...]


def ssd_scan(xs, bc, proj, dt_blk, dt_bias, a_log, h0, direction):
    b, s, d_inner = xs.shape
    n_groups = SSD_GROUPS
    hpg = d_inner // SSD_HEAD_DIM // n_groups
    gw = hpg * SSD_HEAD_DIM
    lc = SSD_CHUNK
    nck = s // lc
    reverse = direction == 1

    def tok(j):
        return nck - 1 - j if reverse else j

    return pl.pallas_call(
        functools.partial(_ssd_kernel, reverse=reverse, direction=direction, nck=nck, n_groups=n_groups, hpg=hpg),
        grid=(b, nck),
        in_specs=[pl.BlockSpec((1, lc, d_inner), lambda bi, j: (bi, tok(j), 0)),
                  pl.BlockSpec((1, lc, bc.shape[2]), lambda bi, j: (bi, tok(j), 0)),
                  pl.BlockSpec((1, lc, LANES), lambda bi, j: (bi, tok(j), dt_blk)),
                  pl.BlockSpec((1, LANES), lambda bi, j: (0, 0)),
                  pl.BlockSpec((1, LANES), lambda bi, j: (0, 0)),
                  pl.BlockSpec((1, n_groups, SSD_STATE, gw), lambda bi, j: (bi, 0, 0, 0))],
        out_specs=[pl.BlockSpec((1, lc, d_inner), lambda bi, j: (bi, tok(j), 0)),
                   pl.BlockSpec((1, n_groups, SSD_STATE, gw), lambda bi, j: (bi, 0, 0, 0))],
        out_shape=[jax.ShapeDtypeStruct((b, s, d_inner), F32),
                   jax.ShapeDtypeStruct((b, n_groups, SSD_STATE, gw), F32)],
        scratch_shapes=[pltpu.VMEM((n_groups, SSD_STATE, gw), F32)],
        compiler_params=_cparams(2),
        name="ssd_scan",
    )(xs, bc, proj, dt_bias, a_log, h0)


def _mixer_gqa(proj_l, proj_c, w_out, q_g, k_g, need_ctx, out_args_l, out_args_c):
    n = proj_l.shape[1]
    c = proj_c.shape[1]
    n_kv = proj_l.shape[2] // HEAD_128 // (GQA_GROUP + 2)
    perm = _deinterleave_perm(HEAD_128, HEAD_128)
    q_gain = q_g[perm].reshape(1, HEAD_128)
    k_gain = k_g[perm].reshape(1, HEAD_128)
    lat_tabs = _rope_tables(n, HEAD_128, 1)
    ctx_tabs = _identity_tables(c)
    o_l = gqa_attention(proj_l, lat_tabs, [(proj_l, lat_tabs), (proj_c, ctx_tabs)], q_gain, k_gain, n_kv)
    x_l = outproj("plain", [o_l], [0], [], w_out, *out_args_l)
    x_c = None
    if need_ctx:
        o_c = gqa_attention(proj_c, ctx_tabs, [(proj_c, ctx_tabs)], q_gain, k_gain, n_kv)
        x_c = outproj("plain", [o_c.reshape(1, -1, o_c.shape[2])], [0], [], w_out, *out_args_c)
    return x_l, x_c


def _mixer_hgrn(proj_l, proj_c, w_out, lb, out_g, need_ctx, out_args_l, out_args_c):
    b, _, w = proj_l.shape
    d = w // HGRN_N_PROJ
    nh = d // HEAD_128
    lb3 = lb.reshape(2, 1, d)
    s0 = jnp.zeros((b, nh, HEAD_128, HEAD_128), F32)
    o_l, o_c = [], []
    for direction in range(2):
        oc, s_ctx = hgrn_scan(proj_c, lb3, s0, direction)
        ol, _ = hgrn_scan(proj_l, lb3, s_ctx, direction)
        o_l.append(ol)
        o_c.append(oc)
    og = jnp.tile(out_g, nh).reshape(1, d)
    g_col0 = 4 * d
    x_l = outproj("hgrn", [o_l[0], o_l[1], proj_l], [0, 0, g_col0], [og], w_out, *out_args_l)
    x_c = None
    if need_ctx:
        flat = lambda t: t.reshape(1, -1, t.shape[2])
        x_c = outproj("hgrn", [flat(o_c[0]), flat(o_c[1]), flat(proj_c)], [0, 0, g_col0], [og], w_out,
                      *out_args_c)
    return x_l, x_c


def _mixer_ssd(proj_l, proj_c, w_out, conv_w, conv_b, dt_bias, a_log, d_skip, norm_g, need_ctx,
               out_args_l, out_args_c):
    b = proj_l.shape[0]
    d_inner = w_out.shape[0]
    n_heads = d_inner // SSD_HEAD_DIM
    hpg = n_heads // SSD_GROUPS
    gn = SSD_GROUPS * SSD_STATE
    tc = 512
    x_blk0 = d_inner // tc
    dt_blk = (2 * d_inner + 2 * gn) // LANES
    cw_x, cw_bc = conv_w[:, :d_inner], conv_w[:, d_inner:]
    cb_x, cb_bc = conv_b[:d_inner].reshape(1, -1), conv_b[d_inner:].reshape(1, -1)
    dtb = dt_bias.reshape(1, 2 * n_heads)
    alog = a_log.reshape(1, 2 * n_heads)
    h0 = jnp.zeros((b, SSD_GROUPS, SSD_STATE, hpg * SSD_HEAD_DIM), F32)

    def prep(proj):
        xs = conv_silu(proj, x_blk0, d_inner, cw_x, cb_x, tc)
        bc = conv_silu(proj, x_blk0 + d_inner // tc, 2 * gn, cw_bc, cb_bc, tc)
        return xs, bc

    xs_l, bc_l = prep(proj_l)
    xs_c, bc_c = prep(proj_c)
    y_l, y_c = [], []
    for direction in range(2):
        yc, h_ctx = ssd_scan(xs_c, bc_c, proj_c, dt_blk, dtb, alog, h0, direction)
        yl, _ = ssd_scan(xs_l, bc_l, proj_l, dt_blk, dtb, alog, h_ctx, direction)
        y_l.append(yl)
        y_c.append(yc)
    dsk = jnp.repeat(d_skip, SSD_HEAD_DIM).reshape(1, d_inner)
    ng = norm_g.reshape(1, d_inner)
    ngrp = d_inner // SSD_GROUPS
    x_l = outproj("ssd", [y_l[0], y_l[1], xs_l, proj_l], [0, 0, 0, 0], [dsk, ng], w_out, *out_args_l,
                  norm_group=ngrp)
    x_c = None
    if need_ctx:
        flat = lambda t: t.reshape(1, -1, t.shape[2])
        x_c = outproj("ssd", [flat(y_c[0]), flat(y_c[1]), flat(xs_c), flat(proj_c)], [0, 0, 0, 0], [dsk, ng],
                      w_out, *out_args_c, norm_group=ngrp)
    return x_l, x_c


def _mixer_diff(proj_l, proj_c, w_out, lam_p, subln_g, lambda_init, out_args_l):
    n = proj_l.shape[1]
    c = proj_c.shape[1]
    lat_tabs = _rope_tables(n, LANES // 2, 2)
    ctx_tabs = _identity_tables(c)
    o_l = diff_attention(proj_l, lat_tabs, [(proj_l, lat_tabs), (proj_c, ctx_tabs)], lam_p,
                         subln_g.reshape(1, LANES), lambda_init)
    return outproj("plain", [o_l], [0], [], w_out, *out_args_l)


def kernel(x, c, ctx, c_ctx, w_mod, b_mod, norm_g, ffn_w13, ffn_w2, attn_w_in, attn_q_g, attn_k_g, attn_w_out, hgrn_w_in, hgrn_lb_logits, hgrn_out_g, hgrn_w_out, ssd_w_in, ssd_conv_w, ssd_conv_b, ssd_dt_bias, ssd_a_log, ssd_d, ssd_norm_g, ssd_w_out, diff_w_in, diff_lambda, diff_subln_g, diff_w_out):
    b, n, d = x.shape
    n_ctx = ctx.shape[1]
    depth = w_mod.shape[0]
    d_ff = ffn_w2.shape[1]
    n_mixers = 4

    n_rows = -(-(b + 1) // 16) * 16
    c_all = jnp.zeros((n_rows, d), F32).at[:b].set(c).at[b].set(c_ctx)
    mods = adaln_all(c_all, w_mod, b_mod).reshape(depth, n_rows, N_MOD, d)

    x_lat = x
    x_ctx = ctx.reshape(1, b * n_ctx, d)
    for layer in range(depth):
        kind, j = layer % n_mixers, layer // n_mixers
        need_ctx = layer < depth - 1
        m3 = mods[layer]
        g = norm_g[layer]
        g_rows = [g[i].reshape(1, d) for i in range(4)]

        if kind == 0:
            perm = _deinterleave_perm(attn_w_in.shape[2], HEAD_128)
            n_qk = (attn_w_in.shape[2] // HEAD_128 // (GQA_GROUP + 2)) * (GQA_GROUP + 1) * HEAD_128
            perm = jnp.where(jnp.arange(perm.shape[0]) < n_qk, perm, jnp.arange(perm.shape[0]))
            w_in = attn_w_in[j][:, perm]
        elif kind == 1:
            w_in = hgrn_w_in[j]
        elif kind == 2:
            w_in = ssd_w_in[j]
        else:
            perm = _deinterleave_perm(diff_w_in.shape[2], LANES // 2)
            n_qk = 2 * (diff_w_in.shape[2] // 3)
            perm = jnp.where(jnp.arange(perm.shape[0]) < n_qk, perm, jnp.arange(perm.shape[0]))
            w_in = diff_w_in[j][:, perm]
        w_in = w_in.astype(BF16)
        n_proj = w_in.shape[1]
        tn = 1152 if n_proj % 1024 else 1024

        proj_l = inproj(x_lat, g_rows[0], m3, 0, 0, w_in, n_proj, tn, F32)
        proj_c = inproj(x_ctx, g_rows[0], m3, b, 0, w_in, n_proj, tn, F32).reshape(b, n_ctx, n_proj)
        out_args_l = (x_lat, g_rows[1], m3, 0, 2)
        out_args_c = (x_ctx, g_rows[1], m3, b, 2)

        if kind == 0:
            x_lat, x_ctx_new = _mixer_gqa(proj_l, proj_c, attn_w_out[j].astype(BF16), attn_q_g[j], attn_k_g[j],
                                          need_ctx, out_args_l, out_args_c)
        elif kind == 1:
            cum = jnp.cumsum(jax.nn.softmax(hgrn_lb_logits.astype(F32), axis=1), axis=1)
            lb = cum[:, layer] - cum[:, 0]
            x_lat, x_ctx_new = _mixer_hgrn(proj_l, proj_c, hgrn_w_out[j].astype(BF16), lb, hgrn_out_g[j],
                                           need_ctx, out_args_l, out_args_c)
        elif kind == 2:
            x_lat, x_ctx_new = _mixer_ssd(proj_l, proj_c, ssd_w_out[j].astype(BF16), ssd_conv_w[j], ssd_conv_b[j],
                                          ssd_dt_bias[j], ssd_a_log[j], ssd_d[j], ssd_norm_g[j], need_ctx,
                                          out_args_l, out_args_c)
        else:
            lambda_init = 0.8 - 0.6 * math.exp(-0.3 * layer)
            x_lat = _mixer_diff(proj_l, proj_c, diff_w_out[j].astype(BF16), diff_lambda[j], diff_subln_g[j],
                                lambda_init, out_args_l)
            x_ctx_new = None

        w13 = ffn_w13[layer].astype(BF16)
        w2 = ffn_w2[layer].astype(BF16)
        u_l = inproj(x_lat, g_rows[2], m3, 0, 3, w13, d_ff, 512, BF16, swiglu=True)
        x_lat = outproj("plain", [u_l], [0], [], w2, x_lat, g_rows[3], m3, 0, 5)
        if need_ctx:
            x_ctx = x_ctx_new
            u_c = inproj(x_ctx, g_rows[2], m3, b, 3, w13, d_ff, 512, BF16, swiglu=True)
            x_ctx = outproj("plain", [u_c], [0], [], w2, x_ctx, g_rows[3], m3, b, 5)
    return x_lat
```

```python
import functools
import math

import jax
import jax.numpy as jnp
from jax import lax
from jax.experimental import pallas as pl
from jax.experimental.pallas import tpu as pltpu

F32 = jnp.float32
BF16 = jnp.bfloat16

NORM_EPS = 1e-6
LOG2_E = math.log2(math.e)
ROPE_THETA = 10000.0
GRID_W = 64
N_MOD = 6
HEAD_128 = 128
GQA_GROUP = 4
HGRN_CHUNK = 32
HGRN_N_PROJ = 5
SSD_HEAD_DIM = 64
SSD_GROUPS = 8
SSD_STATE = 128
DIFF_HEADS = 16

LANES = 128
VMEM_LIMIT_BYTES = 56 * 2**20
TM_IN = 1024
TM_OUT = 512
ROW_CHUNK = 16
RESIDENT_K_FULL_TILE = 4096
READOUT_SPLIT = 2
ROW_UNROLL = 8
TQ_GQA = 512
TQ_DIFF = 1024
HGRN_BLOCK = 256
HGRN_HEADS_PER_STEP = 8
SSD_CHUNK = 128


def _cparams(n_axes):
    return pltpu.CompilerParams(dimension_semantics=("arbitrary",) * n_axes,
                                vmem_limit_bytes=VMEM_LIMIT_BYTES)


def _silu(x):
    return x / (1.0 + jnp.exp(-x))


def _dot(a, b):
    return jnp.dot(a, b, preferred_element_type=F32)


def _dot_nt(a, b):
    return lax.dot_general(a, b, (((1,), (1,)), ((), ())), preferred_element_type=F32)


def _dot_tn(a, b):
    return lax.dot_general(a, b, (((0,), (0,)), ((), ())), preferred_element_type=F32)


def _dot_exact_lhs(m01, x):
    hi = x.astype(BF16)
    r1 = x - hi.astype(F32)
    mid = r1.astype(BF16)
    lo = (r1 - mid.astype(F32)).astype(BF16)
    return _dot(m01, hi) + _dot(m01, mid) + _dot(m01, lo)


def _adaln_kernel(c_ref, w_ref, b_ref, o_ref):
    s = _silu(c_ref[...]).astype(BF16)
    o_ref[0] = _dot(s, w_ref[0].astype(BF16)) + b_ref[0]


def adaln_all(c_all, w_mod, b_mod):
    n_layers, d, n = w_mod.shape
    r = c_all.shape[0]
    tn = 1024
    return pl.pallas_call(
        _adaln_kernel,
        grid=(n_layers, n // tn),
        in_specs=[pl.BlockSpec((r, d), lambda l, j: (0, 0)),
                  pl.BlockSpec((1, d, tn), lambda l, j: (l, 0, j)),
                  pl.BlockSpec((1, 1, tn), lambda l, j: (l, 0, j))],
        out_specs=pl.BlockSpec((1, r, tn), lambda l, j: (l, 0, j)),
        out_shape=jax.ShapeDtypeStruct((n_layers, r, n), F32),
        compiler_params=_cparams(2),
        name="adaln",
    )(c_all, w_mod, b_mod.reshape(n_layers, 1, n))


def _inproj_kernel(x_ref, g_ref, m_ref, *rest, shift_row, n_w, swiglu):
    w_refs = rest[:n_w]
    o_ref = rest[n_w]
    h_ref = rest[n_w + 1]

    @pl.when(pl.program_id(2) == 0)
    def _():
        def rows(i, carry):
            r0 = pl.multiple_of(i * ROW_CHUNK, ROW_CHUNK)
            x = x_ref[0, pl.ds(r0, ROW_CHUNK), :]
            ms = jnp.mean(x * x, axis=-1, keepdims=True)
            y = x * lax.rsqrt(ms + NORM_EPS) * g_ref[...]
            shift = m_ref[0, shift_row:shift_row + 1, :]
            scale = m_ref[0, shift_row + 1:shift_row + 2, :]
            h_ref[pl.ds(r0, ROW_CHUNK), :] = (y * (1.0 + scale) + shift).astype(BF16)
            return carry

        lax.fori_loop(0, h_ref.shape[0] // ROW_CHUNK, rows, 0, unroll=ROW_UNROLL)

    h = h_ref[...]
    if swiglu:
        half = o_ref.shape[2] // 2
        for c in range(2):
            cs = slice(c * half, (c + 1) * half)
            gate = _dot(h, w_refs[0][:, cs])
            up = _dot(h, w_refs[1][:, cs])
            o_ref[0, :, cs] = (_silu(gate) * up).astype(o_ref.dtype)
    else:
        o_ref[0] = _dot(h, w_refs[0][...]).astype(o_ref.dtype)


def inproj(x3, g_row, mods3, mod_row0, shift_row, w, n_out, tn, out_dtype, swiglu=False):
    bx, s, d = x3.shape
    tm = min(TM_IN, s)
    offs = (0, n_out // tn) if swiglu else (0,)
    in_specs = [pl.BlockSpec((1, tm, d), lambda b, i, j: (b, i, 0)),
                pl.BlockSpec((1, d), lambda b, i, j: (0, 0)),
                pl.BlockSpec((1, N_MOD, d), lambda b, i, j: (b + mod_row0, 0, 0))]
    for off in offs:
        in_specs.append(pl.BlockSpec((d, tn), lambda b, i, j, off=off: (0, j + off)))
    return pl.pallas_call(
        functools.partial(_inproj_kernel, shift_row=shift_row, n_w=len(offs), swiglu=swiglu),
        grid=(bx, s // tm, n_out // tn),
        in_specs=in_specs,
        out_specs=pl.BlockSpec((1, tm, tn), lambda b, i, j: (b, i, j)),
        out_shape=jax.ShapeDtypeStruct((bx, s, n_out), out_dtype),
        scratch_shapes=[pltpu.VMEM((tm, d), BF16)],
        compiler_params=_cparams(3),
        name="inproj_swiglu" if swiglu else "inproj",
    )(x3, g_row, mods3, *([w] * len(offs)))


def _outproj_kernel(*refs, mode, gate_row, nk, norm_group):
    n_in = {"plain": 1, "hgrn": 4, "ssd": 6}[mode]
    ins = refs[:n_in]
    w_ref, res_ref, g_ref, m_ref, o_ref, acc_ref = refs[n_in:]
    k = pl.program_id(2)

    def readout(rs):
        if mode == "plain":
            return ins[0][0, rs, :]
        if mode == "hgrn":
            of_ref, ob_ref, gp_ref, og_ref = ins
            o = of_ref[0, rs, :] + ob_ref[0, rs, :]
            pieces = []
            for h in range(o.shape[-1] // HEAD_128):
                sl = slice(h * HEAD_128, (h + 1) * HEAD_128)
                oh = o[:, sl]
                ms = jnp.mean(oh * oh, axis=-1, keepdims=True)
                pieces.append(oh * lax.rsqrt(ms + NORM_EPS) * og_ref[:, sl])
            return (jnp.concatenate(pieces, axis=-1) * _silu(gp_ref[0, rs, :])).astype(BF16)
        yf_ref, yb_ref, xs_ref, z_ref, dsk_ref, ng_ref = ins
        pieces = []
        for gi in range(xs_ref.shape[2] // norm_group):
            sl = slice(gi * norm_group, (gi + 1) * norm_group)
            y = dsk_ref[:, sl] * xs_ref[0, rs, sl] + yf_ref[0, rs, sl] + yb_ref[0, rs, sl]
            y = y * _silu(z_ref[0, rs, sl])
            ms = jnp.mean(y * y, axis=-1, keepdims=True)
            pieces.append((y * lax.rsqrt(ms + NORM_EPS) * ng_ref[:, sl]).astype(BF16))
        return pieces[0] if len(pieces) == 1 else jnp.concatenate(pieces, axis=-1)

    if nk > 1:
        @pl.when(k == 0)
        def _():
            acc_ref[...] = jnp.zeros_like(acc_ref)

    tm = acc_ref.shape[0]
    n_split = 1 if mode == "plain" else READOUT_SPLIT
    for s in range(n_split):
        rs = slice(s * tm // n_split, (s + 1) * tm // n_split)
        part = _dot(readout(rs), w_ref[...])
        if nk == 1:
            acc_ref[rs, :] = part
        else:
            acc_ref[rs, :] += part

    @pl.when(k == nk - 1)
    def _():
        def rows(i, carry):
            r0 = pl.multiple_of(i * ROW_CHUNK, ROW_CHUNK)
            y = acc_ref[pl.ds(r0, ROW_CHUNK), :]
            ms = jnp.mean(y * y, axis=-1, keepdims=True)
            n = y * lax.rsqrt(ms + NORM_EPS) * g_ref[...]
            o_ref[0, pl.ds(r0, ROW_CHUNK), :] = (res_ref[0, pl.ds(r0, ROW_CHUNK), :]
                                                 + m_ref[0, gate_row:gate_row + 1, :] * n)
            return carry

        lax.fori_loop(0, acc_ref.shape[0] // ROW_CHUNK, rows, 0, unroll=ROW_UNROLL)


def _outproj_tk(kdim, cap):
    for nk in range(1, kdim // LANES + 1):
        if kdim % nk == 0 and (kdim // nk) % LANES == 0 and kdim // nk <= cap:
            return kdim // nk
    return LANES


def outproj(mode, ins, in_cols, row_ins, w, res3, g_row, mods3, mod_row0, gate_row, tk=None, norm_group=0):
    bx, s, d = res3.shape
    kdim = w.shape[0]
    resident = mode == "plain"
    if resident:
        tk = kdim
        tm = min(TM_OUT if kdim <= RESIDENT_K_FULL_TILE else TM_OUT // 2, s)
    else:
        tk = tk or _outproj_tk(kdim, 1024)
        tm = min(TM_OUT, s)
    nk = kdim // tk
    in_col_blocks = [c // tk for c in in_cols]
    in_specs = [pl.BlockSpec((1, tm, tk), lambda b, i, k, off=off: (b, i, k + off)) for off in in_col_blocks]
    in_specs += [pl.BlockSpec((1, tk), lambda b, i, k: (0, k)) for _ in row_ins]
    w_mode = dict(pipeline_mode=pl.Buffered(1)) if resident else {}
    in_specs += [pl.BlockSpec((tk, d), lambda b, i, k: (k, 0), **w_mode),
                 pl.BlockSpec((1, tm, d), lambda b, i, k: (b, i, 0)),
                 pl.BlockSpec((1, d), lambda b, i, k: (0, 0)),
                 pl.BlockSpec((1, N_MOD, d), lambda b, i, k: (b + mod_row0, 0, 0))]
    return pl.pallas_call(
        functools.partial(_outproj_kernel, mode=mode, gate_row=gate_row, nk=nk, norm_group=norm_group),
        grid=(bx, s // tm, nk),
        in_specs=in_specs,
        out_specs=pl.BlockSpec((1, tm, d), lambda b, i, k: (b, i, 0)),
        out_shape=jax.ShapeDtypeStruct((bx, s, d), F32),
        scratch_shapes=[pltpu.VMEM((tm, d), F32)],
        compiler_params=_cparams(3),
        name="outproj_" + mode,
    )(*ins, *row_ins, w, res3, g_row, mods3)


def _deinterleave_perm(width, block):
    idx = jnp.arange(width).reshape(width // block, block)
    return jnp.concatenate([idx[:, 0::2], idx[:, 1::2]], axis=1).reshape(width)


def _rope_tables(n, head_dim, reps):
    rows = n // GRID_W
    row = jnp.repeat(jnp.arange(rows, dtype=F32), GRID_W)
    col = jnp.tile(jnp.arange(GRID_W, dtype=F32), rows)
    axis_dim = head_dim // 2
    inv_freq = ROPE_THETA ** (-jnp.arange(0, axis_dim, 2, dtype=F32) / axis_dim)
    ang = jnp.concatenate([row[:, None] * inv_freq, col[:, None] * inv_freq], axis=-1)
    cos, sin = jnp.cos(ang), jnp.sin(ang)
    cos_t = jnp.tile(jnp.concatenate([cos, cos], axis=-1), (1, reps))
    sin_t = jnp.tile(jnp.concatenate([-sin, sin], axis=-1), (1, reps))
    return cos_t, sin_t


def _identity_tables(n):
    return jnp.ones((n, LANES), F32), jnp.zeros((n, LANES), F32)


def _rot_half_128(y):
    return pltpu.roll(y, 64, 1)


def _rot_half_64(y):
    lane = lax.broadcasted_iota(jnp.int32, y.shape, 1)
    first = jnp.bitwise_and(lane, 63) < 32
    return jnp.where(first, pltpu.roll(y, 96, 1), pltpu.roll(y, 32, 1))


def _col_reduce(x, op):
    n, t = x.shape
    if n > LANES:
        x = op(x.reshape(n // LANES, LANES, t), axis=0)
    return op(x, axis=0, keepdims=True)


def _softmax_numerators_t(scores):
    m = functools.reduce(jnp.maximum, [_col_reduce(sc, jnp.max) for sc in scores])
    ps = [jnp.exp2(sc - m) for sc in scores]
    l = functools.reduce(jnp.add, [_col_reduce(p, jnp.sum) for p in ps])
    return ps, l


def _softmax_pv_t(scores, v_t):
    ps, l = _softmax_numerators_t(scores)
    o_t = functools.reduce(jnp.add, [_dot(v, p.astype(BF16)) for v, p in zip(v_t, ps)])
    return o_t, l


def _gqa_norm_rope(x, gain, cos, sin):
    ms = jnp.mean(x * x, axis=-1, keepdims=True)
    y = x * lax.rsqrt(ms + NORM_EPS) * gain
    return y * cos + _rot_half_128(y) * sin


def _gqa_kernel(q_ref, cq_ref, sq_ref, qg_ref, kg_ref, *rest, n_seg, group, scale):
    seg_refs = rest[:4 * n_seg]
    o_ref = rest[4 * n_seg]
    scr = rest[4 * n_seg + 1:]

    @pl.when(pl.program_id(2) == 0)
    def _():
        for s in range(n_seg):
            k_ref, v_ref, ck_ref, sk_ref = seg_refs[4 * s:4 * s + 4]
            scr[2 * s][...] = _gqa_norm_rope(k_ref[0], kg_ref[...], ck_ref[...], sk_ref[...]).astype(BF16)
            scr[2 * s + 1][...] = v_ref[0].T.astype(BF16)

    heads = [slice(g * HEAD_128, (g + 1) * HEAD_128) for g in range(group)]
    qn = [(_gqa_norm_rope(q_ref[0, :, sl], qg_ref[...], cq_ref[...], sq_ref[...]) * (scale * LOG2_E)).astype(BF16)
          for sl in heads]
    def qk(g):
        return [_dot_nt(scr[2 * s][...], qn[g]) for s in range(n_seg)]

    scores = qk(0)
    for g, sl in enumerate(heads):
        nxt = qk(g + 1) if g + 1 < group else None
        o_t, l = _softmax_pv_t(scores, [scr[2 * s + 1][...] for s in range(n_seg)])
        o_ref[0, :, sl] = (o_t * (1.0 / l)).T.astype(o_ref.dtype)
        scores = nxt


def gqa_attention(q_src, q_tabs, segs, q_gain, k_gain, n_kv):
    b, sq, _ = q_src.shape
    group = GQA_GROUP
    tq = min(TQ_GQA, sq)
    qw = group * HEAD_128
    k_blk0 = n_kv * group
    v_blk0 = k_blk0 + n_kv
    in_specs = [pl.BlockSpec((1, tq, qw), lambda bi, h, i: (bi, i, h)),
                pl.BlockSpec((tq, LANES), lambda bi, h, i: (i, 0)),
                pl.BlockSpec((tq, LANES), lambda bi, h, i: (i, 0)),
                pl.BlockSpec((1, LANES), lambda bi, h, i: (0, 0)),
                pl.BlockSpec((1, LANES), lambda bi, h, i: (0, 0))]
    args = [q_src, q_tabs[0], q_tabs[1], q_gain, k_gain]
    scratch = []
    for kv_src, (ck, sk) in segs:
        ks = kv_src.shape[1]
        in_specs += [pl.BlockSpec((1, ks, LANES), lambda bi, h, i: (bi, 0, k_blk0 + h)),
                     pl.BlockSpec((1, ks, LANES), lambda bi, h, i: (bi, 0, v_blk0 + h)),
                     pl.BlockSpec((ks, LANES), lambda bi, h, i: (0, 0)),
                     pl.BlockSpec((ks, LANES), lambda bi, h, i: (0, 0))]
        args += [kv_src, kv_src, ck, sk]
        scratch += [pltpu.VMEM((ks, LANES), BF16), pltpu.VMEM((LANES, ks), BF16)]
    return pl.pallas_call(
        functools.partial(_gqa_kernel, n_seg=len(segs), group=group, scale=HEAD_128 ** -0.5),
        grid=(b, n_kv, sq // tq),
        in_specs=in_specs,
        out_specs=pl.BlockSpec((1, tq, qw), lambda bi, h, i: (bi, i, h)),
        out_shape=jax.ShapeDtypeStruct((b, sq, n_kv * qw), BF16),
        scratch_shapes=scratch,
        compiler_params=_cparams(3),
        name="gqa_attention",
    )(*args)


def _diff_kernel(q_ref, cq_ref, sq_ref, lam_ref, sg_ref, *rest, n_seg, scale, lambda_init):
    seg_refs = rest[:4 * n_seg]
    o_ref = rest[4 * n_seg]
    scr = rest[4 * n_seg + 1:]

    @pl.when(pl.program_id(2) == 0)
    def _():
        for s in range(n_seg):
            k_ref, v_ref, ck_ref, sk_ref = seg_refs[4 * s:4 * s + 4]
            k = k_ref[0]
            scr[2 * s][...] = (k * ck_ref[...] + _rot_half_64(k) * sk_ref[...]).astype(BF16)
            scr[2 * s + 1][...] = v_ref[0].T.astype(BF16)

    q = q_ref[0]
    q = (q * cq_ref[...] + _rot_half_64(q) * sq_ref[...]) * (scale * LOG2_E)
    lane = lax.broadcasted_iota(jnp.int32, q.shape, 1)
    zero = jnp.zeros_like(q)
    qs = [jnp.where(lane < 64, q, zero).astype(BF16), jnp.where(lane < 64, zero, q).astype(BF16)]

    lp = lam_ref[...]
    lam = (jnp.exp(jnp.sum(lp[0:1] * lp[1:2], axis=-1, keepdims=True))
           - jnp.exp(jnp.sum(lp[2:3] * lp[3:4], axis=-1, keepdims=True)) + lambda_init)

    scores = [[_dot_nt(scr[2 * s][...], qc) for s in range(n_seg)] for qc in qs]
    (p1, l1), (p2, l2) = [_softmax_numerators_t(sc) for sc in scores]
    r = lam * l1 / l2
    o_t = functools.reduce(jnp.add, [
        _dot(scr[2 * s + 1][...], (p1[s] - r * p2[s]).astype(BF16)) for s in range(n_seg)])
    o = (o_t * (1.0 / l1)).T
    ms = jnp.mean(o * o, axis=-1, keepdims=True)
    o_ref[0] = (o * lax.rsqrt(ms + NORM_EPS) * sg_ref[...] * (1.0 - lambda_init)).astype(o_ref.dtype)


def diff_attention(q_src, q_tabs, segs, lam_p, subln_g, lambda_init):
    b, sq, _ = q_src.shape
    nh = DIFF_HEADS
    tq = min(TQ_DIFF, sq)
    in_specs = [pl.BlockSpec((1, tq, LANES), lambda bi, h, i: (bi, i, h)),
                pl.BlockSpec((tq, LANES), lambda bi, h, i: (i, 0)),
                pl.BlockSpec((tq, LANES), lambda bi, h, i: (i, 0)),
                pl.BlockSpec(lam_p.shape, lambda bi, h, i: (0, 0)),
                pl.BlockSpec((1, LANES), lambda bi, h, i: (0, 0))]
    args = [q_src, q_tabs[0], q_tabs[1], lam_p, subln_g]
    scratch = []
    for kv_src, (ck, sk) in segs:
        ks = kv_src.shape[1]
        in_specs += [pl.BlockSpec((1, ks, LANES), lambda bi, h, i: (bi, 0, nh + h)),
                     pl.BlockSpec((1, ks, LANES), lambda bi, h, i: (bi, 0, 2 * nh + h)),
                     pl.BlockSpec((ks, LANES), lambda bi, h, i: (0, 0)),
                     pl.BlockSpec((ks, LANES), lambda bi, h, i: (0, 0))]
        args += [kv_src, kv_src, ck, sk]
        scratch += [pltpu.VMEM((ks, LANES), BF16), pltpu.VMEM((LANES, ks), BF16)]
    return pl.pallas_call(
        functools.partial(_diff_kernel, n_seg=len(segs), scale=(LANES // 2) ** -0.5, lambda_init=lambda_init),
        grid=(b, nh, sq // tq),
        in_specs=in_specs,
        out_specs=pl.BlockSpec((1, tq, LANES), lambda bi, h, i: (bi, i, h)),
        out_shape=jax.ShapeDtypeStruct((b, sq, nh * LANES), BF16),
        scratch_shapes=scratch,
        compiler_params=_cparams(3),
        name="diff_attention",
    )(*args)


def _hgrn_kernel(q_ref, f_ref, v_ref, lb_ref, s0_ref, o_ref, sout_ref, st_ref, *, reverse, hpb, blk_len, nblk):
    blk = pl.program_id(2)

    @pl.when(blk == 0)
    def _():
        st_ref[...] = s0_ref[0]

    n_chunks = blk_len // HGRN_CHUNK
    row = lax.broadcasted_iota(jnp.int32, (blk_len, blk_len), 0)
    col = lax.broadcasted_iota(jnp.int32, (blk_len, blk_len), 1)
    same = lax.shift_right_logical(row, 5) == lax.shift_right_logical(col, 5)
    tri = jnp.logical_and(same, (col >= row) if reverse else (col <= row))
    tri01 = tri.astype(F32).astype(BF16)
    heads = [slice(h * HEAD_128, (h + 1) * HEAD_128) for h in range(hpb)]
    chunks = [slice(c * HGRN_CHUNK, (c + 1) * HGRN_CHUNK) for c in range(n_chunks)]

    q = _silu(q_ref[0])
    lb = lb_ref[0]
    forget = lb + (1.0 - lb) * (1.0 / (1.0 + jnp.exp(-f_ref[0])))
    k = 1.0 - forget
    cum = _dot_exact_lhs(tri01, jnp.log(forget))
    last_rows = [cum[c.start:c.start + 1] if reverse else cum[c.stop - 1:c.stop] for c in chunks]
    last = jnp.concatenate([jnp.broadcast_to(r, (HGRN_CHUNK, r.shape[1])) for r in last_rows], axis=0)
    q_dec = (q * jnp.exp(cum)).astype(BF16)
    k_inv = (k * jnp.exp(-cum)).astype(BF16)
    k_end = (k * jnp.exp(last - cum)).astype(BF16)
    vb = v_ref[0].astype(BF16)
    e_last = [jnp.exp(r) for r in last_rows]
    att = [jnp.where(tri, _dot_nt(q_dec[:, sl], k_inv[:, sl]), 0.0).astype(BF16) for sl in heads]
    o_intra = [_dot(att[h], vb[:, sl]) for h, sl in enumerate(heads)]
    upd = [[_dot_tn(vb[rs, sl], k_end[rs, sl]) for sl in heads] for rs in chunks]
    st = [st_ref[h] for h in range(hpb)]
    for ci in range(n_chunks):
        c = n_chunks - 1 - ci if reverse else ci
        rs = chunks[c]
        for h, sl in enumerate(heads):
            o_ref[0, rs, sl] = o_intra[h][rs] + _dot_nt(q_dec[rs, sl], st[h].astype(BF16))
            st[h] = st[h] * e_last[c][:, sl] + upd[c][h]
    for h in range(hpb):
        st_ref[h] = st[h]

    @pl.when(blk == nblk - 1)
    def _():
        sout_ref[0] = st_ref[...]


def hgrn_scan(proj, lb3, s0, direction):
    b, s, w = proj.shape
    d = w // HGRN_N_PROJ
    nh = d // HEAD_128
    hpb = HGRN_HEADS_PER_STEP
    hw = hpb * HEAD_128
    blk_len = min(HGRN_BLOCK, s)
    nblk = s // blk_len
    ncb = d // hw
    reverse = direction == 1

    def tok(j):
        return nblk - 1 - j if reverse else j

    in_specs = [pl.BlockSpec((1, blk_len, hw), lambda bi, h, j: (bi, tok(j), h)),
                pl.BlockSpec((1, blk_len, hw), lambda bi, h, j: (bi, tok(j), (1 + direction) * ncb + h)),
                pl.BlockSpec((1, blk_len, hw), lambda bi, h, j: (bi, tok(j), 3 * ncb + h)),
                pl.BlockSpec((1, 1, hw), lambda bi, h, j: (direction, 0, h)),
                pl.BlockSpec((1, hpb, HEAD_128, HEAD_128), lambda bi, h, j: (bi, h, 0, 0))]
    return pl.pallas_call(
        functools.partial(_hgrn_kernel, reverse=reverse, hpb=hpb, blk_len=blk_len, nblk=nblk),
        grid=(b, nh // hpb, nblk),
        in_specs=in_specs,
        out_specs=[pl.BlockSpec((1, blk_len, hw), lambda bi, h, j: (bi, tok(j), h)),
                   pl.BlockSpec((1, hpb, HEAD_128, HEAD_128), lambda bi, h, j: (bi, h, 0, 0))],
        out_shape=[jax.ShapeDtypeStruct((b, s, d), F32),
                   jax.ShapeDtypeStruct((b, nh, HEAD_128, HEAD_128), F32)],
        scratch_shapes=[pltpu.VMEM((hpb, HEAD_128, HEAD_128), F32)],
        compiler_params=_cparams(3),
        name="hgrn_scan",
    )(proj, proj, proj, lb3, s0)


def _conv_silu_kernel(x_ref, w_ref, b_ref, o_ref):
    u = x_ref[0]
    n = u.shape[0]
    row = lax.broadcasted_iota(jnp.int32, u.shape, 0)
    prev = jnp.where(row == 0, 0.0, pltpu.roll(u, 1, 0))
    nxt = jnp.where(row == n - 1, 0.0, pltpu.roll(u, n - 1, 0))
    y = prev * w_ref[0:1, :] + u * w_ref[1:2, :] + nxt * w_ref[2:3, :] + b_ref[...]
    o_ref[0] = _silu(y)


def conv_silu(proj, col_blk0, width, conv_w, conv_b, tc=512):
    b, s, _ = proj.shape
    return pl.pallas_call(
        _conv_silu_kernel,
        grid=(b, width // tc),
        in_specs=[pl.BlockSpec((1, s, tc), lambda bi, j: (bi, 0, col_blk0 + j)),
                  pl.BlockSpec((3, tc), lambda bi, j: (0, j)),
                  pl.BlockSpec((1, tc), lambda bi, j: (0, j))],
        out_specs=pl.BlockSpec((1, s, tc), lambda bi, j: (bi, 0, j)),
        out_shape=jax.ShapeDtypeStruct((b, s, width), F32),
        compiler_params=_cparams(2),
        name="ssd_conv_silu",
    )(proj, conv_w, conv_b)


def _ssd_kernel(x_ref, bc_ref, dt_ref, dtb_ref, alog_ref, h0_ref, y_ref, hout_ref, st_ref,
                *, reverse, direction, nck, n_groups, hpg):
    ck = pl.program_id(1)
    lc = SSD_CHUNK
    gw = hpg * SSD_HEAD_DIM
    n_heads = n_groups * hpg

    @pl.when(ck == 0)
    def _():
        st_ref[...] = h0_ref[0]

    row = lax.broadcasted_iota(jnp.int32, (lc, lc), 0)
    col = lax.broadcasted_iota(jnp.int32, (lc, lc), 1)
    tri = (col >= row) if reverse else (col <= row)
    tri01 = tri.astype(F32).astype(BF16)
    lo = lax.broadcasted_iota(jnp.int32, (lc, LANES), 1) < SSD_HEAD_DIM
    lo_row = lo[0:1, :]

    x_in = dt_ref[0] + dtb_ref[...]
    dt = jnp.maximum(x_in, 0.0) + jnp.log1p(jnp.exp(-jnp.abs(x_in)))
    a = -jnp.exp(alog_ref[...])
    cum = _dot_exact_lhs(tri01, dt * a)
    cum_t = cum.T
    dt_t = dt.T
    t_last = 0 if reverse else lc - 1

    for g in range(n_groups):
        b_g = bc_ref[0, :, g * SSD_STATE:(g + 1) * SSD_STATE]
        c_g = bc_ref[0, :, (n_groups + g) * SSD_STATE:(n_groups + g + 1) * SSD_STATE].astype(BF16)
        b_t = b_g.T
        cb = _dot_nt(c_g, b_g.astype(BF16))
        h_g = st_ref[g]
        y_state = _dot(c_g, h_g.astype(BF16))
        for p in range(hpg // 2):
            x2 = x_ref[0, :, g * gw + p * LANES:g * gw + (p + 1) * LANES]
            zero = jnp.zeros_like(x2)
            x_halves = [jnp.where(lo, x2, zero).astype(BF16), jnp.where(lo, zero, x2).astype(BF16)]
            y_parts, upd_parts, e_cols, decays = [], [], [], []
            for jj in range(2):
                r = direction * n_heads + g * hpg + 2 * p + jj
                row_b = jnp.broadcast_to(cum_t[r:r + 1, :], (lc, lc))
                col_b = row_b.T
                dt_row = dt_t[r:r + 1, :]
                decay = jnp.exp(jnp.where(tri, col_b - row_b, -jnp.inf))
                w = (cb * decay * dt_row).astype(BF16)
                y_parts.append(_dot(w, x_halves[jj]))
                last_b = col_b[t_last:t_last + 1, :]
                coef_row = jnp.exp(last_b - row_b[0:1, :]) * dt_row
                upd_parts.append(_dot((b_t * coef_row).astype(BF16), x_halves[jj]))
                e_cols.append(jnp.exp(col_b))
                decays.append(jnp.exp(last_b))
            cs = slice(p * LANES, (p + 1) * LANES)
            y_ref[0, :, g * gw + p * LANES:g * gw + (p + 1) * LANES] = (
                y_parts[0] + y_parts[1] + y_state[:, cs] * jnp.where(lo, e_cols[0], e_cols[1]))
            st_ref[g, :, cs] = (h_g[:, cs] * jnp.where(lo_row, decays[0], decays[1])
                                + upd_parts[0] + upd_parts[1])

    @pl.when(ck == nck - 1)
    def _():
        hout_ref[0] = st_ref[...]


def ssd_scan(xs, bc, proj, dt_blk, dt_bias, a_log, h0, direction):
    b, s, d_inner = xs.shape
    n_groups = SSD_GROUPS
    hpg = d_inner // SSD_HEAD_DIM // n_groups
    gw = hpg * SSD_HEAD_DIM
    lc = SSD_CHUNK
    nck = s // lc
    reverse = direction == 1

    def tok(j):
        return nck - 1 - j if reverse else j

    return pl.pallas_call(
        functools.partial(_ssd_kernel, reverse=reverse, direction=direction, nck=nck, n_groups=n_groups, hpg=hpg),
        grid=(b, nck),
        in_specs=[pl.BlockSpec((1, lc, d_inner), lambda bi, j: (bi, tok(j), 0)),
                  pl.BlockSpec((1, lc, bc.shape[2]), lambda bi, j: (bi, tok(j), 0)),
                  pl.BlockSpec((1, lc, LANES), lambda bi, j: (bi, tok(j), dt_blk)),
                  pl.BlockSpec((1, LANES), lambda bi, j: (0, 0)),
                  pl.BlockSpec((1, LANES), lambda bi, j: (0, 0)),
                  pl.BlockSpec((1, n_groups, SSD_STATE, gw), lambda bi, j: (bi, 0, 0, 0))],
        out_specs=[pl.BlockSpec((1, lc, d_inner), lambda bi, j: (bi, tok(j), 0)),
                   pl.BlockSpec((1, n_groups, SSD_STATE, gw), lambda bi, j: (bi, 0, 0, 0))],
        out_shape=[jax.ShapeDtypeStruct((b, s, d_inner), F32),
                   jax.ShapeDtypeStruct((b, n_groups, SSD_STATE, gw), F32)],
        scratch_shapes=[pltpu.VMEM((n_groups, SSD_STATE, gw), F32)],
        compiler_params=_cparams(2),
        name="ssd_scan",
    )(xs, bc, proj, dt_bias, a_log, h0)


def _mixer_gqa(proj_l, proj_c, w_out, q_g, k_g, need_ctx, out_args_l, out_args_c):
    n = proj_l.shape[1]
    c = proj_c.shape[1]
    n_kv = proj_l.shape[2] // HEAD_128 // (GQA_GROUP + 2)
    perm = _deinterleave_perm(HEAD_128, HEAD_128)
    q_gain = q_g[perm].reshape(1, HEAD_128)
    k_gain = k_g[perm].reshape(1, HEAD_128)
    lat_tabs = _rope_tables(n, HEAD_128, 1)
    ctx_tabs = _identity_tables(c)
    o_l = gqa_attention(proj_l, lat_tabs, [(proj_l, lat_tabs), (proj_c, ctx_tabs)], q_gain, k_gain, n_kv)
    x_l = outproj("plain", [o_l], [0], [], w_out, *out_args_l)
    x_c = None
    if need_ctx:
        o_c = gqa_attention(proj_c, ctx_tabs, [(proj_c, ctx_tabs)], q_gain, k_gain, n_kv)
        x_c = outproj("plain", [o_c.reshape(1, -1, o_c.shape[2])], [0], [], w_out, *out_args_c)
    return x_l, x_c


def _mixer_hgrn(proj_l, proj_c, w_out, lb, out_g, need_ctx, out_args_l, out_args_c):
    b, _, w = proj_l.shape
    d = w // HGRN_N_PROJ
    nh = d // HEAD_128
    lb3 = lb.reshape(2, 1, d)
    s0 = jnp.zeros((b, nh, HEAD_128, HEAD_128), F32)
    o_l, o_c = [], []
    for direction in range(2):
        oc, s_ctx = hgrn_scan(proj_c, lb3, s0, direction)
        ol, _ = hgrn_scan(proj_l, lb3, s_ctx, direction)
        o_l.append(ol)
        o_c.append(oc)
    og = jnp.tile(out_g, nh).reshape(1, d)
    g_col0 = 4 * d
    x_l = outproj("hgrn", [o_l[0], o_l[1], proj_l], [0, 0, g_col0], [og], w_out, *out_args_l)
    x_c = None
    if need_ctx:
        flat = lambda t: t.reshape(1, -1, t.shape[2])
        x_c = outproj("hgrn", [flat(o_c[0]), flat(o_c[1]), flat(proj_c)], [0, 0, g_col0], [og], w_out,
                      *out_args_c)
    return x_l, x_c


def _mixer_ssd(proj_l, proj_c, w_out, conv_w, conv_b, dt_bias, a_log, d_skip, norm_g, need_ctx,
               out_args_l, out_args_c):
    b = proj_l.shape[0]
    d_inner = w_out.shape[0]
    n_heads = d_inner // SSD_HEAD_DIM
    hpg = n_heads // SSD_GROUPS
    gn = SSD_GROUPS * SSD_STATE
    tc = 512
    x_blk0 = d_inner // tc
    dt_blk = (2 * d_inner + 2 * gn) // LANES
    cw_x, cw_bc = conv_w[:, :d_inner], conv_w[:, d_inner:]
    cb_x, cb_bc = conv_b[:d_inner].reshape(1, -1), conv_b[d_inner:].reshape(1, -1)
    dtb = dt_bias.reshape(1, 2 * n_heads)
    alog = a_log.reshape(1, 2 * n_heads)
    h0 = jnp.zeros((b, SSD_GROUPS, SSD_STATE, hpg * SSD_HEAD_DIM), F32)

    def prep(proj):
        xs = conv_silu(proj, x_blk0, d_inner, cw_x, cb_x, tc)
        bc = conv_silu(proj, x_blk0 + d_inner // tc, 2 * gn, cw_bc, cb_bc, tc)
        return xs, bc

    xs_l, bc_l = prep(proj_l)
    xs_c, bc_c = prep(proj_c)
    y_l, y_c = [], []
    for direction in range(2):
        yc, h_ctx = ssd_scan(xs_c, bc_c, proj_c, dt_blk, dtb, alog, h0, direction)
        yl, _ = ssd_scan(xs_l, bc_l, proj_l, dt_blk, dtb, alog, h_ctx, direction)
        y_l.append(yl)
        y_c.append(yc)
    dsk = jnp.repeat(d_skip, SSD_HEAD_DIM).reshape(1, d_inner)
    ng = norm_g.reshape(1, d_inner)
    ngrp = d_inner // SSD_GROUPS
    x_l = outproj("ssd", [y_l[0], y_l[1], xs_l, proj_l], [0, 0, 0, 0], [dsk, ng], w_out, *out_args_l,
                  norm_group=ngrp)
    x_c = None
    if need_ctx:
        flat = lambda t: t.reshape(1, -1, t.shape[2])
        x_c = outproj("ssd", [flat(y_c[0]), flat(y_c[1]), flat(xs_c), flat(proj_c)], [0, 0, 0, 0], [dsk, ng],
                      w_out, *out_args_c, norm_group=ngrp)
    return x_l, x_c


def _mixer_diff(proj_l, proj_c, w_out, lam_p, subln_g, lambda_init, out_args_l):
    n = proj_l.shape[1]
    c = proj_c.shape[1]
    lat_tabs = _rope_tables(n, LANES // 2, 2)
    ctx_tabs = _identity_tables(c)
    o_l = diff_attention(proj_l, lat_tabs, [(proj_l, lat_tabs), (proj_c, ctx_tabs)], lam_p,
                         subln_g.reshape(1, LANES), lambda_init)
    return outproj("plain", [o_l], [0], [], w_out, *out_args_l)


def kernel(x, c, ctx, c_ctx, w_mod, b_mod, norm_g, ffn_w13, ffn_w2, attn_w_in, attn_q_g, attn_k_g, attn_w_out, hgrn_w_in, hgrn_lb_logits, hgrn_out_g, hgrn_w_out, ssd_w_in, ssd_conv_w, ssd_conv_b, ssd_dt_bias, ssd_a_log, ssd_d, ssd_norm_g, ssd_w_out, diff_w_in, diff_lambda, diff_subln_g, diff_w_out):
    b, n, d = x.shape
    n_ctx = ctx.shape[1]
    depth = w_mod.shape[0]
    d_ff = ffn_w2.shape[1]
    n_mixers = 4

    n_rows = -(-(b + 1) // 16) * 16
    c_all = jnp.zeros((n_rows, d), F32).at[:b].set(c).at[b].set(c_ctx)
    mods = adaln_all(c_all, w_mod, b_mod).reshape(depth, n_rows, N_MOD, d)

    x_lat = x
    x_ctx = ctx.reshape(1, b * n_ctx, d)
    for layer in range(depth):
        kind, j = layer % n_mixers, layer // n_mixers
        need_ctx = layer < depth - 1
        m3 = mods[layer]
        g = norm_g[layer]
        g_rows = [g[i].reshape(1, d) for i in range(4)]

        if kind == 0:
            perm = _deinterleave_perm(attn_w_in.shape[2], HEAD_128)
            n_qk = (attn_w_in.shape[2] // HEAD_128 // (GQA_GROUP + 2)) * (GQA_GROUP + 1) * HEAD_128
            perm = jnp.where(jnp.arange(perm.shape[0]) < n_qk, perm, jnp.arange(perm.shape[0]))
            w_in = attn_w_in[j][:, perm]
        elif kind == 1:
            w_in = hgrn_w_in[j]
        elif kind == 2:
            w_in = ssd_w_in[j]
        else:
            perm = _deinterleave_perm(diff_w_in.shape[2], LANES // 2)
            n_qk = 2 * (diff_w_in.shape[2] // 3)
            perm = jnp.where(jnp.arange(perm.shape[0]) < n_qk, perm, jnp.arange(perm.shape[0]))
            w_in = diff_w_in[j][:, perm]
        w_in = w_in.astype(BF16)
        n_proj = w_in.shape[1]
        tn = 1152 if n_proj % 1024 else 1024

        proj_l = inproj(x_lat, g_rows[0], m3, 0, 0, w_in, n_proj, tn, F32)
        proj_c = inproj(x_ctx, g_rows[0], m3, b, 0, w_in, n_proj, tn, F32).reshape(b, n_ctx, n_proj)
        out_args_l = (x_lat, g_rows[1], m3, 0, 2)
        out_args_c = (x_ctx, g_rows[1], m3, b, 2)

        if kind == 0:
            x_lat, x_ctx_new = _mixer_gqa(proj_l, proj_c, attn_w_out[j].astype(BF16), attn_q_g[j], attn_k_g[j],
                                          need_ctx, out_args_l, out_args_c)
        elif kind == 1:
            cum = jnp.cumsum(jax.nn.softmax(hgrn_lb_logits.astype(F32), axis=1), axis=1)
            lb = cum[:, layer] - cum[:, 0]
            x_lat, x_ctx_new = _mixer_hgrn(proj_l, proj_c, hgrn_w_out[j].astype(BF16), lb, hgrn_out_g[j],
                                           need_ctx, out_args_l, out_args_c)
        elif kind == 2:
            x_lat, x_ctx_new = _mixer_ssd(proj_l, proj_c, ssd_w_out[j].astype(BF16), ssd_conv_w[j], ssd_conv_b[j],
                                          ssd_dt_bias[j], ssd_a_log[j], ssd_d[j], ssd_norm_g[j], need_ctx,
                                          out_args_l, out_args_c)
        else:
            lambda_init = 0.8 - 0.6 * math.exp(-0.3 * layer)
            x_lat = _mixer_diff(proj_l, proj_c, diff_w_out[j].astype(BF16), diff_lambda[j], diff_subln_g[j],
                                lambda_init, out_args_l)
            x_ctx_new = None

        w13 = ffn_w13[layer].astype(BF16)
        w2 = ffn_w2[layer].astype(BF16)
        u_l = inproj(x_lat, g_rows[2], m3, 0, 3, w13, d_ff, 512, BF16, swiglu=True)
        x_lat = outproj("plain", [u_l], [0], [], w2, x_lat, g_rows[3], m3, 0, 5)
        if need_ctx:
            x_ctx = x_ctx_new
            u_c = inproj(x_ctx, g_rows[2], m3, b, 3, w13, d_ff, 512, BF16, swiglu=True)
            x_ctx = outproj("plain", [u_c], [0], [], w2, x_ctx, g_rows[3], m3, b, 5)
    return x_lat
```

```python
import functools
import math

import jax
import jax.numpy as jnp
from jax import lax
from jax.experimental import pallas as pl
from jax.experimental.pallas import tpu as pltpu

F32 = jnp.float32
BF16 = jnp.bfloat16

NORM_EPS = 1e-6
LOG2_E = math.log2(math.e)
ROPE_THETA = 10000.0
GRID_W = 64
N_MOD = 6
HEAD_128 = 128
GQA_GROUP = 4
HGRN_CHUNK = 32
HGRN_N_PROJ = 5
SSD_HEAD_DIM = 64
SSD_GROUPS = 8
SSD_STATE = 128
DIFF_HEADS = 16

LANES = 128
VMEM_LIMIT_BYTES = 56 * 2**20
TM_IN = 1024
TM_OUT = 512
ROW_CHUNK = 16
RESIDENT_K_FULL_TILE = 4096
READOUT_SPLIT = 2
ROW_UNROLL = 8
TQ_GQA = 512
TQ_DIFF = 1024
HGRN_BLOCK = 256
HGRN_HEADS_PER_STEP = 8
SSD_CHUNK = 128


def _cparams(n_axes):
    return pltpu.CompilerParams(dimension_semantics=("arbitrary",) * n_axes,
                                vmem_limit_bytes=VMEM_LIMIT_BYTES)


def _silu(x):
    return x / (1.0 + jnp.exp(-x))


def _dot(a, b):
    return jnp.dot(a, b, preferred_element_type=F32)


def _dot_nt(a, b):
    return lax.dot_general(a, b, (((1,), (1,)), ((), ())), preferred_element_type=F32)


def _dot_tn(a, b):
    return lax.dot_general(a, b, (((0,), (0,)), ((), ())), preferred_element_type=F32)


def _dot_exact_lhs(m01, x):
    hi = x.astype(BF16)
    r1 = x - hi.astype(F32)
    mid = r1.astype(BF16)
    lo = (r1 - mid.astype(F32)).astype(BF16)
    return _dot(m01, hi) + _dot(m01, mid) + _dot(m01, lo)


def _adaln_kernel(c_ref, w_ref, b_ref, o_ref):
    s = _silu(c_ref[...]).astype(BF16)
    o_ref[0] = _dot(s, w_ref[0].astype(BF16)) + b_ref[0]


def adaln_all(c_all, w_mod, b_mod):
    n_layers, d, n = w_mod.shape
    r = c_all.shape[0]
    tn = 1024
    return pl.pallas_call(
        _adaln_kernel,
        grid=(n_layers, n // tn),
        in_specs=[pl.BlockSpec((r, d), lambda l, j: (0, 0)),
                  pl.BlockSpec((1, d, tn), lambda l, j: (l, 0, j)),
                  pl.BlockSpec((1, 1, tn), lambda l, j: (l, 0, j))],
        out_specs=pl.BlockSpec((1, r, tn), lambda l, j: (l, 0, j)),
        out_shape=jax.ShapeDtypeStruct((n_layers, r, n), F32),
        compiler_params=_cparams(2),
        name="adaln",
    )(c_all, w_mod, b_mod.reshape(n_layers, 1, n))


def _inproj_kernel(x_ref, g_ref, m_ref, *rest, shift_row, n_w, swiglu):
    w_refs = rest[:n_w]
    o_ref = rest[n_w]
    h_ref = rest[n_w + 1]

    @pl.when(pl.program_id(2) == 0)
    def _():
        def rows(i, carry):
            r0 = pl.multiple_of(i * ROW_CHUNK, ROW_CHUNK)
            x = x_ref[0, pl.ds(r0, ROW_CHUNK), :]
            ms = jnp.mean(x * x, axis=-1, keepdims=True)
            y = x * lax.rsqrt(ms + NORM_EPS) * g_ref[...]
            shift = m_ref[0, shift_row:shift_row + 1, :]
            scale = m_ref[0, shift_row + 1:shift_row + 2, :]
            h_ref[pl.ds(r0, ROW_CHUNK), :] = (y * (1.0 + scale) + shift).astype(BF16)
            return carry

        lax.fori_loop(0, h_ref.shape[0] // ROW_CHUNK, rows, 0, unroll=ROW_UNROLL)

    h = h_ref[...]
    if swiglu:
        half = o_ref.shape[2] // 2
        for c in range(2):
            cs = slice(c * half, (c + 1) * half)
            gate = _dot(h, w_refs[0][:, cs])
            up = _dot(h, w_refs[1][:, cs])
            o_ref[0, :, cs] = (_silu(gate) * up).astype(o_ref.dtype)
    else:
        o_ref[0] = _dot(h, w_refs[0][...]).astype(o_ref.dtype)


def inproj(x3, g_row, mods3, mod_row0, shift_row, w, n_out, tn, out_dtype, swiglu=False):
    bx, s, d = x3.shape
    tm = min(TM_IN, s)
    offs = (0, n_out // tn) if swiglu else (0,)
    in_specs = [pl.BlockSpec((1, tm, d), lambda b, i, j: (b, i, 0)),
                pl.BlockSpec((1, d), lambda b, i, j: (0, 0)),
                pl.BlockSpec((1, N_MOD, d), lambda b, i, j: (b + mod_row0, 0, 0))]
    for off in offs:
        in_specs.append(pl.BlockSpec((d, tn), lambda b, i, j, off=off: (0, j + off)))
    return pl.pallas_call(
        functools.partial(_inproj_kernel, shift_row=shift_row, n_w=len(offs), swiglu=swiglu),
        grid=(bx, s // tm, n_out // tn),
        in_specs=in_specs,
        out_specs=pl.BlockSpec((1, tm, tn), lambda b, i, j: (b, i, j)),
        out_shape=jax.ShapeDtypeStruct((bx, s, n_out), out_dtype),
        scratch_shapes=[pltpu.VMEM((tm, d), BF16)],
        compiler_params=_cparams(3),
        name="inproj_swiglu" if swiglu else "inproj",
    )(x3, g_row, mods3, *([w] * len(offs)))


def _outproj_kernel(*refs, mode, gate_row, norm_group):
    n_in = {"plain": 1, "hgrn": 4, "ssd": 3}[mode]
    ins = refs[:n_in]
    w_ref, res_ref, g_ref, m_ref, o_ref, acc_ref = refs[n_in:]

    def readout(rs):
        if mode == "plain":
            return ins[0][0, rs, :]
        if mode == "hgrn":
            of_ref, ob_ref, gp_ref, og_ref = ins
            o = of_ref[0, rs, :] + ob_ref[0, rs, :]
            pieces = []
            for h in range(o.shape[-1] // HEAD_128):
                sl = slice(h * HEAD_128, (h + 1) * HEAD_128)
                oh = o[:, sl]
                ms = jnp.mean(oh * oh, axis=-1, keepdims=True)
                pieces.append(oh * lax.rsqrt(ms + NORM_EPS) * og_ref[:, sl])
            return (jnp.concatenate(pieces, axis=-1) * _silu(gp_ref[0, rs, :])).astype(BF16)
        y_ref, z_ref, ng_ref = ins
        pieces = []
        for gi in range(y_ref.shape[2] // norm_group):
            sl = slice(gi * norm_group, (gi + 1) * norm_group)
            y = y_ref[0, rs, sl] * _silu(z_ref[0, rs, sl])
            ms = jnp.mean(y * y, axis=-1, keepdims=True)
            pieces.append((y * lax.rsqrt(ms + NORM_EPS) * ng_ref[:, sl]).astype(BF16))
        return pieces[0] if len(pieces) == 1 else jnp.concatenate(pieces, axis=-1)

    tm = acc_ref.shape[0]
    n_split = 1 if mode == "plain" else READOUT_SPLIT
    for s in range(n_split):
        rs = slice(s * tm // n_split, (s + 1) * tm // n_split)
        acc_ref[rs, :] = _dot(readout(rs), w_ref[...])

    def rows(i, carry):
        r0 = pl.multiple_of(i * ROW_CHUNK, ROW_CHUNK)
        y = acc_ref[pl.ds(r0, ROW_CHUNK), :]
        ms = jnp.mean(y * y, axis=-1, keepdims=True)
        n = y * lax.rsqrt(ms + NORM_EPS) * g_ref[...]
        o_ref[0, pl.ds(r0, ROW_CHUNK), :] = (res_ref[0, pl.ds(r0, ROW_CHUNK), :]
                                             + m_ref[0, gate_row:gate_row + 1, :] * n)
        return carry

    lax.fori_loop(0, tm // ROW_CHUNK, rows, 0, unroll=ROW_UNROLL)


def outproj(mode, ins, in_cols, row_ins, w, res3, g_row, mods3, mod_row0, gate_row, norm_group=0):
    bx, s, d = res3.shape
    kdim = w.shape[0]
    tm = min(TM_OUT if (mode == "plain" and kdim <= RESIDENT_K_FULL_TILE) else TM_OUT // 2, s)
    in_specs = [pl.BlockSpec((1, tm, kdim), lambda b, i, off=c // kdim: (b, i, off)) for c in in_cols]
    in_specs += [pl.BlockSpec((1, kdim), lambda b, i: (0, 0)) for _ in row_ins]
    in_specs += [pl.BlockSpec((kdim, d), lambda b, i: (0, 0), pipeline_mode=pl.Buffered(1)),
                 pl.BlockSpec((1, tm, d), lambda b, i: (b, i, 0)),
                 pl.BlockSpec((1, d), lambda b, i: (0, 0)),
                 pl.BlockSpec((1, N_MOD, d), lambda b, i: (b + mod_row0, 0, 0))]
    return pl.pallas_call(
        functools.partial(_outproj_kernel, mode=mode, gate_row=gate_row, norm_group=norm_group),
        grid=(bx, s // tm),
        in_specs=in_specs,
        out_specs=pl.BlockSpec((1, tm, d), lambda b, i: (b, i, 0)),
        out_shape=jax.ShapeDtypeStruct((bx, s, d), F32),
        scratch_shapes=[pltpu.VMEM((tm, d), F32)],
        compiler_params=_cparams(2),
        name="outproj_" + mode,
    )(*ins, *row_ins, w, res3, g_row, mods3)


def _deinterleave_perm(width, block):
    idx = jnp.arange(width).reshape(width // block, block)
    return jnp.concatenate([idx[:, 0::2], idx[:, 1::2]], axis=1).reshape(width)


def _rope_tables(n, head_dim, reps):
    rows = n // GRID_W
    row = jnp.repeat(jnp.arange(rows, dtype=F32), GRID_W)
    col = jnp.tile(jnp.arange(GRID_W, dtype=F32), rows)
    axis_dim = head_dim // 2
    inv_freq = ROPE_THETA ** (-jnp.arange(0, axis_dim, 2, dtype=F32) / axis_dim)
    ang = jnp.concatenate([row[:, None] * inv_freq, col[:, None] * inv_freq], axis=-1)
    cos, sin = jnp.cos(ang), jnp.sin(ang)
    cos_t = jnp.tile(jnp.concatenate([cos, cos], axis=-1), (1, reps))
    sin_t = jnp.tile(jnp.concatenate([-sin, sin], axis=-1), (1, reps))
    return cos_t, sin_t


def _identity_tables(n):
    return jnp.ones((n, LANES), F32), jnp.zeros((n, LANES), F32)


def _rot_half_128(y):
    return pltpu.roll(y, 64, 1)


def _rot_half_64(y):
    lane = lax.broadcasted_iota(jnp.int32, y.shape, 1)
    first = jnp.bitwise_and(lane, 63) < 32
    return jnp.where(first, pltpu.roll(y, 96, 1), pltpu.roll(y, 32, 1))


def _col_reduce(x, op):
    n, t = x.shape
    if n > LANES:
        x = op(x.reshape(n // LANES, LANES, t), axis=0)
    return op(x, axis=0, keepdims=True)


def _softmax_numerators_t(scores):
    m = functools.reduce(jnp.maximum, [_col_reduce(sc, jnp.max) for sc in scores])
    ps = [jnp.exp2(sc - m) for sc in scores]
    l = functools.reduce(jnp.add, [_col_reduce(p, jnp.sum) for p in ps])
    return ps, l


def _softmax_pv_t(scores, v_t):
    ps, l = _softmax_numerators_t(scores)
    o_t = functools.reduce(jnp.add, [_dot(v, p.astype(BF16)) for v, p in zip(v_t, ps)])
    return o_t, l


def _gqa_norm_rope(x, gain, cos, sin):
    ms = jnp.mean(x * x, axis=-1, keepdims=True)
    y = x * lax.rsqrt(ms + NORM_EPS) * gain
    return y * cos + _rot_half_128(y) * sin


def _gqa_kernel(q_ref, cq_ref, sq_ref, qg_ref, kg_ref, *rest, n_seg, group, scale):
    seg_refs = rest[:4 * n_seg]
    o_ref = rest[4 * n_seg]
    scr = rest[4 * n_seg + 1:]

    @pl.when(pl.program_id(2) == 0)
    def _():
        for s in range(n_seg):
            k_ref, v_ref, ck_ref, sk_ref = seg_refs[4 * s:4 * s + 4]
            scr[2 * s][...] = _gqa_norm_rope(k_ref[0], kg_ref[...], ck_ref[...], sk_ref[...]).astype(BF16)
            scr[2 * s + 1][...] = v_ref[0].T.astype(BF16)

    heads = [slice(g * HEAD_128, (g + 1) * HEAD_128) for g in range(group)]
    qn = [(_gqa_norm_rope(q_ref[0, :, sl], qg_ref[...], cq_ref[...], sq_ref[...]) * (scale * LOG2_E)).astype(BF16)
          for sl in heads]
    def qk(g):
        return [_dot_nt(scr[2 * s][...], qn[g]) for s in range(n_seg)]

    scores = qk(0)
    for g, sl in enumerate(heads):
        nxt = qk(g + 1) if g + 1 < group else None
        o_t, l = _softmax_pv_t(scores, [scr[2 * s + 1][...] for s in range(n_seg)])
        o_ref[0, :, sl] = (o_t * (1.0 / l)).T.astype(o_ref.dtype)
        scores = nxt


def gqa_attention(q_src, q_tabs, segs, q_gain, k_gain, n_kv):
    b, sq, _ = q_src.shape
    group = GQA_GROUP
    tq = min(TQ_GQA, sq)
    qw = group * HEAD_128
    k_blk0 = n_kv * group
    v_blk0 = k_blk0 + n_kv
    in_specs = [pl.BlockSpec((1, tq, qw), lambda bi, h, i: (bi, i, h)),
                pl.BlockSpec((tq, LANES), lambda bi, h, i: (i, 0)),
                pl.BlockSpec((tq, LANES), lambda bi, h, i: (i, 0)),
                pl.BlockSpec((1, LANES), lambda bi, h, i: (0, 0)),
                pl.BlockSpec((1, LANES), lambda bi, h, i: (0, 0))]
    args = [q_src, q_tabs[0], q_tabs[1], q_gain, k_gain]
    scratch = []
    for kv_src, (ck, sk) in segs:
        ks = kv_src.shape[1]
        in_specs += [pl.BlockSpec((1, ks, LANES), lambda bi, h, i: (bi, 0, k_blk0 + h)),
                     pl.BlockSpec((1, ks, LANES), lambda bi, h, i: (bi, 0, v_blk0 + h)),
                     pl.BlockSpec((ks, LANES), lambda bi, h, i: (0, 0)),
                     pl.BlockSpec((ks, LANES), lambda bi, h, i: (0, 0))]
        args += [kv_src, kv_src, ck, sk]
        scratch += [pltpu.VMEM((ks, LANES), BF16), pltpu.VMEM((LANES, ks), BF16)]
    return pl.pallas_call(
        functools.partial(_gqa_kernel, n_seg=len(segs), group=group, scale=HEAD_128 ** -0.5),
        grid=(b, n_kv, sq // tq),
        in_specs=in_specs,
        out_specs=pl.BlockSpec((1, tq, qw), lambda bi, h, i: (bi, i, h)),
        out_shape=jax.ShapeDtypeStruct((b, sq, n_kv * qw), BF16),
        scratch_shapes=scratch,
        compiler_params=_cparams(3),
        name="gqa_attention",
    )(*args)


def _diff_kernel(q_ref, cq_ref, sq_ref, lam_ref, sg_ref, *rest, n_seg, scale, lambda_init):
    seg_refs = rest[:4 * n_seg]
    o_ref = rest[4 * n_seg]
    scr = rest[4 * n_seg + 1:]

    @pl.when(pl.program_id(2) == 0)
    def _():
        for s in range(n_seg):
            k_ref, v_ref, ck_ref, sk_ref = seg_refs[4 * s:4 * s + 4]
            k = k_ref[0]
            scr[2 * s][...] = (k * ck_ref[...] + _rot_half_64(k) * sk_ref[...]).astype(BF16)
            scr[2 * s + 1][...] = v_ref[0].T.astype(BF16)

    q = q_ref[0]
    q = (q * cq_ref[...] + _rot_half_64(q) * sq_ref[...]) * (scale * LOG2_E)
    lane = lax.broadcasted_iota(jnp.int32, q.shape, 1)
    zero = jnp.zeros_like(q)
    qs = [jnp.where(lane < 64, q, zero).astype(BF16), jnp.where(lane < 64, zero, q).astype(BF16)]

    lp = lam_ref[...]
    lam = (jnp.exp(jnp.sum(lp[0:1] * lp[1:2], axis=-1, keepdims=True))
           - jnp.exp(jnp.sum(lp[2:3] * lp[3:4], axis=-1, keepdims=True)) + lambda_init)

    scores = [[_dot_nt(scr[2 * s][...], qc) for s in range(n_seg)] for qc in qs]
    (p1, l1), (p2, l2) = [_softmax_numerators_t(sc) for sc in scores]
    r = lam * l1 / l2
    o_t = functools.reduce(jnp.add, [
        _dot(scr[2 * s + 1][...], (p1[s] - r * p2[s]).astype(BF16)) for s in range(n_seg)])
    o = (o_t * (1.0 / l1)).T
    ms = jnp.mean(o * o, axis=-1, keepdims=True)
    o_ref[0] = (o * lax.rsqrt(ms + NORM_EPS) * sg_ref[...] * (1.0 - lambda_init)).astype(o_ref.dtype)


def diff_attention(q_src, q_tabs, segs, lam_p, subln_g, lambda_init):
    b, sq, _ = q_src.shape
    nh = DIFF_HEADS
    tq = min(TQ_DIFF, sq)
    in_specs = [pl.BlockSpec((1, tq, LANES), lambda bi, h, i: (bi, i, h)),
                pl.BlockSpec((tq, LANES), lambda bi, h, i: (i, 0)),
                pl.BlockSpec((tq, LANES), lambda bi, h, i: (i, 0)),
                pl.BlockSpec(lam_p.shape, lambda bi, h, i: (0, 0)),
                pl.BlockSpec((1, LANES), lambda bi, h, i: (0, 0))]
    args = [q_src, q_tabs[0], q_tabs[1], lam_p, subln_g]
    scratch = []
    for kv_src, (ck, sk) in segs:
        ks = kv_src.shape[1]
        in_specs += [pl.BlockSpec((1, ks, LANES), lambda bi, h, i: (bi, 0, nh + h)),
                     pl.BlockSpec((1, ks, LANES), lambda bi, h, i: (bi, 0, 2 * nh + h)),
                     pl.BlockSpec((ks, LANES), lambda bi, h, i: (0, 0)),
                     pl.BlockSpec((ks, LANES), lambda bi, h, i: (0, 0))]
        args += [kv_src, kv_src, ck, sk]
        scratch += [pltpu.VMEM((ks, LANES), BF16), pltpu.VMEM((LANES, ks), BF16)]
    return pl.pallas_call(
        functools.partial(_diff_kernel, n_seg=len(segs), scale=(LANES // 2) ** -0.5, lambda_init=lambda_init),
        grid=(b, nh, sq // tq),
        in_specs=in_specs,
        out_specs=pl.BlockSpec((1, tq, LANES), lambda bi, h, i: (bi, i, h)),
        out_shape=jax.ShapeDtypeStruct((b, sq, nh * LANES), BF16),
        scratch_shapes=scratch,
        compiler_params=_cparams(3),
        name="diff_attention",
    )(*args)


def _hgrn_kernel(q_ref, f_ref, v_ref, lb_ref, s0_ref, o_ref, sout_ref, st_ref, *, reverse, hpb, blk_len, nblk):
    blk = pl.program_id(2)

    @pl.when(blk == 0)
    def _():
        st_ref[...] = s0_ref[0]

    n_chunks = blk_len // HGRN_CHUNK
    row = lax.broadcasted_iota(jnp.int32, (blk_len, blk_len), 0)
    col = lax.broadcasted_iota(jnp.int32, (blk_len, blk_len), 1)
    same = lax.shift_right_logical(row, 5) == lax.shift_right_logical(col, 5)
    tri = jnp.logical_and(same, (col >= row) if reverse else (col <= row))
    tri01 = tri.astype(F32).astype(BF16)
    heads = [slice(h * HEAD_128, (h + 1) * HEAD_128) for h in range(hpb)]
    chunks = [slice(c * HGRN_CHUNK, (c + 1) * HGRN_CHUNK) for c in range(n_chunks)]

    q = _silu(q_ref[0])
    lb = lb_ref[0]
    forget = lb + (1.0 - lb) * (1.0 / (1.0 + jnp.exp(-f_ref[0])))
    k = 1.0 - forget
    cum = _dot_exact_lhs(tri01, jnp.log(forget))
    last_rows = [cum[c.start:c.start + 1] if reverse else cum[c.stop - 1:c.stop] for c in chunks]
    last = jnp.concatenate([jnp.broadcast_to(r, (HGRN_CHUNK, r.shape[1])) for r in last_rows], axis=0)
    q_dec = (q * jnp.exp(cum)).astype(BF16)
    k_inv = (k * jnp.exp(-cum)).astype(BF16)
    k_end = (k * jnp.exp(last - cum)).astype(BF16)
    vb = v_ref[0].astype(BF16)
    e_last = [jnp.exp(r) for r in last_rows]
    att = [jnp.where(tri, _dot_nt(q_dec[:, sl], k_inv[:, sl]), 0.0).astype(BF16) for sl in heads]
    o_intra = [_dot(att[h], vb[:, sl]) for h, sl in enumerate(heads)]
    upd = [[_dot_tn(vb[rs, sl], k_end[rs, sl]) for sl in heads] for rs in chunks]
    st = [st_ref[h] for h in range(hpb)]
    for ci in range(n_chunks):
        c = n_chunks - 1 - ci if reverse else ci
        rs = chunks[c]
        for h, sl in enumerate(heads):
            o_ref[0, rs, sl] = o_intra[h][rs] + _dot_nt(q_dec[rs, sl], st[h].astype(BF16))
            st[h] = st[h] * e_last[c][:, sl] + upd[c][h]
    for h in range(hpb):
        st_ref[h] = st[h]

    @pl.when(blk == nblk - 1)
    def _():
        sout_ref[0] = st_ref[...]


def hgrn_scan(proj, lb3, s0, direction):
    b, s, w = proj.shape
    d = w // HGRN_N_PROJ
    nh = d // HEAD_128
    hpb = HGRN_HEADS_PER_STEP
    hw = hpb * HEAD_128
    blk_len = min(HGRN_BLOCK, s)
    nblk = s // blk_len
    ncb = d // hw
    reverse = direction == 1

    def tok(j):
        return nblk - 1 - j if reverse else j

    in_specs = [pl.BlockSpec((1, blk_len, hw), lambda bi, h, j: (bi, tok(j), h)),
                pl.BlockSpec((1, blk_len, hw), lambda bi, h, j: (bi, tok(j), (1 + direction) * ncb + h)),
                pl.BlockSpec((1, blk_len, hw), lambda bi, h, j: (bi, tok(j), 3 * ncb + h)),
                pl.BlockSpec((1, 1, hw), lambda bi, h, j: (direction, 0, h)),
                pl.BlockSpec((1, hpb, HEAD_128, HEAD_128), lambda bi, h, j: (bi, h, 0, 0))]
    return pl.pallas_call(
        functools.partial(_hgrn_kernel, reverse=reverse, hpb=hpb, blk_len=blk_len, nblk=nblk),
        grid=(b, nh // hpb, nblk),
        in_specs=in_specs,
        out_specs=[pl.BlockSpec((1, blk_len, hw), lambda bi, h, j: (bi, tok(j), h)),
                   pl.BlockSpec((1, hpb, HEAD_128, HEAD_128), lambda bi, h, j: (bi, h, 0, 0))],
        out_shape=[jax.ShapeDtypeStruct((b, s, d), F32),
                   jax.ShapeDtypeStruct((b, nh, HEAD_128, HEAD_128), F32)],
        scratch_shapes=[pltpu.VMEM((hpb, HEAD_128, HEAD_128), F32)],
        compiler_params=_cparams(3),
        name="hgrn_scan",
    )(proj, proj, proj, lb3, s0)


def _conv_silu_kernel(x_ref, w_ref, b_ref, o_ref):
    u = x_ref[0]
    n = u.shape[0]
    row = lax.broadcasted_iota(jnp.int32, u.shape, 0)
    prev = jnp.where(row == 0, 0.0, pltpu.roll(u, 1, 0))
    nxt = jnp.where(row == n - 1, 0.0, pltpu.roll(u, n - 1, 0))
    y = prev * w_ref[0:1, :] + u * w_ref[1:2, :] + nxt * w_ref[2:3, :] + b_ref[...]
    o_ref[0] = _silu(y).astype(o_ref.dtype)


def conv_silu(proj, col_blk0, width, conv_w, conv_b, tc=512):
    b, s, _ = proj.shape
    return pl.pallas_call(
        _conv_silu_kernel,
        grid=(b, width // tc),
        in_specs=[pl.BlockSpec((1, s, tc), lambda bi, j: (bi, 0, col_blk0 + j)),
                  pl.BlockSpec((3, tc), lambda bi, j: (0, j)),
                  pl.BlockSpec((1, tc), lambda bi, j: (0, j))],
        out_specs=pl.BlockSpec((1, s, tc), lambda bi, j: (bi, 0, j)),
        out_shape=jax.ShapeDtypeStruct((b, s, width), BF16),
        compiler_params=_cparams(2),
        name="ssd_conv_silu",
    )(proj, conv_w, conv_b)


def _ssd_kernel(x_ref, bc_ref, dt_ref, dtb_ref, alog_ref, dsk_ref, h0_ref, *rest,
                reverse, direction, nck, n_groups, hpg, has_prev):
    yprev_ref = rest[0] if has_prev else None
    y_ref, hout_ref, st_ref = rest[-3:]
    ck = pl.program_id(1)
    lc = SSD_CHUNK
    gw = hpg * SSD_HEAD_DIM
    n_heads = n_groups * hpg

    @pl.when(ck == 0)
    def _():
        st_ref[...] = h0_ref[0]

    row = lax.broadcasted_iota(jnp.int32, (lc, lc), 0)
    col = lax.broadcasted_iota(jnp.int32, (lc, lc), 1)
    tri = (col >= row) if reverse else (col <= row)
    tri01 = tri.astype(F32).astype(BF16)
    lo = lax.broadcasted_iota(jnp.int32, (lc, LANES), 1) < SSD_HEAD_DIM
    lo_row = lo[0:1, :]

    x_in = dt_ref[0] + dtb_ref[...]
    dt = jnp.maximum(x_in, 0.0) + jnp.log1p(jnp.exp(-jnp.abs(x_in)))
    a = -jnp.exp(alog_ref[...])
    cum = _dot_exact_lhs(tri01, dt * a)
    cum_t = cum.T
    dt_t = dt.T
    t_last = 0 if reverse else lc - 1

    for g in range(n_groups):
        b_g = bc_ref[0, :, g * SSD_STATE:(g + 1) * SSD_STATE]
        c_g = bc_ref[0, :, (n_groups + g) * SSD_STATE:(n_groups + g + 1) * SSD_STATE]
        b_t = b_g.astype(F32).T
        cb = _dot_nt(c_g, b_g)
        h_g = st_ref[g]
        y_state = _dot(c_g, h_g.astype(BF16))
        for p in range(hpg // 2):
            cols = slice(g * gw + p * LANES, g * gw + (p + 1) * LANES)
            x2 = x_ref[0, :, cols].astype(F32)
            zero = jnp.zeros_like(x2)
            x_halves = [jnp.where(lo, x2, zero).astype(BF16), jnp.where(lo, zero, x2).astype(BF16)]
            y_parts, upd_parts, e_cols, decays = [], [], [], []
            for jj in range(2):
                r = direction * n_heads + g * hpg + 2 * p + jj
                row_b = jnp.broadcast_to(cum_t[r:r + 1, :], (lc, lc))
                col_b = row_b.T
                dt_row = dt_t[r:r + 1, :]
                decay = jnp.exp(jnp.where(tri, col_b - row_b, -jnp.inf))
                w = (cb * decay * dt_row).astype(BF16)
                y_parts.append(_dot(w, x_halves[jj]))
                last_b = col_b[t_last:t_last + 1, :]
                coef_row = jnp.exp(last_b - row_b[0:1, :]) * dt_row
                upd_parts.append(_dot((b_t * coef_row).astype(BF16), x_halves[jj]))
                e_cols.append(jnp.exp(col_b))
                decays.append(jnp.exp(last_b))
            cs = slice(p * LANES, (p + 1) * LANES)
            base = yprev_ref[0, :, cols] if has_prev else dsk_ref[:, cols] * x2
            y_ref[0, :, cols] = base + (
                y_parts[0] + y_parts[1] + y_state[:, cs] * jnp.where(lo, e_cols[0], e_cols[1]))
            st_ref[g, :, cs] = (h_g[:, cs] * jnp.where(lo_row, decays[0], decays[1])
                                + upd_parts[0] + upd_parts[1])

    @pl.when(ck == nck - 1)
    def _():
        hout_ref[0] = st_ref[...]


def ssd_scan(xs, bc, proj, dt_blk, dt_bias, a_log, d_skip, h0, y_prev, direction):
    b, s, d_inner = xs.shape
    has_prev = y_prev is not None
    n_groups = SSD_GROUPS
    hpg = d_inner // SSD_HEAD_DIM // n_groups
    gw = hpg * SSD_HEAD_DIM
    lc = SSD_CHUNK
    nck = s // lc
    reverse = direction == 1

    def tok(j):
        return nck - 1 - j if reverse else j

    seq_spec = pl.BlockSpec((1, lc, d_inner), lambda bi, j: (bi, tok(j), 0))
    return pl.pallas_call(
        functools.partial(_ssd_kernel, reverse=reverse, direction=direction, nck=nck, n_groups=n_groups, hpg=hpg,
                          has_prev=has_prev),
        grid=(b, nck),
        in_specs=[seq_spec,
                  pl.BlockSpec((1, lc, bc.shape[2]), lambda bi, j: (bi, tok(j), 0)),
                  pl.BlockSpec((1, lc, LANES), lambda bi, j: (bi, tok(j), dt_blk)),
                  pl.BlockSpec((1, LANES), lambda bi, j: (0, 0)),
                  pl.BlockSpec((1, LANES), lambda bi, j: (0, 0)),
                  pl.BlockSpec((1, d_inner), lambda bi, j: (0, 0)),
                  pl.BlockSpec((1, n_groups, SSD_STATE, gw), lambda bi, j: (bi, 0, 0, 0))]
        + ([seq_spec] if has_prev else []),
        out_specs=[pl.BlockSpec((1, lc, d_inner), lambda bi, j: (bi, tok(j), 0)),
                   pl.BlockSpec((1, n_groups, SSD_STATE, gw), lambda bi, j: (bi, 0, 0, 0))],
        out_shape=[jax.ShapeDtypeStruct((b, s, d_inner), F32),
                   jax.ShapeDtypeStruct((b, n_groups, SSD_STATE, gw), F32)],
        scratch_shapes=[pltpu.VMEM((n_groups, SSD_STATE, gw), F32)],
        compiler_params=_cparams(2),
        name="ssd_scan",
    )(xs, bc, proj, dt_bias, a_log, d_skip, h0, *([y_prev] if has_prev else []))


def _mixer_gqa(proj_l, proj_c, w_out, q_g, k_g, need_ctx, out_args_l, out_args_c):
    n = proj_l.shape[1]
    c = proj_c.shape[1]
    n_kv = proj_l.shape[2] // HEAD_128 // (GQA_GROUP + 2)
    perm = _deinterleave_perm(HEAD_128, HEAD_128)
    q_gain = q_g[perm].reshape(1, HEAD_128)
    k_gain = k_g[perm].reshape(1, HEAD_128)
    lat_tabs = _rope_tables(n, HEAD_128, 1)
    ctx_tabs = _identity_tables(c)
    o_l = gqa_attention(proj_l, lat_tabs, [(proj_l, lat_tabs), (proj_c, ctx_tabs)], q_gain, k_gain, n_kv)
    x_l = outproj("plain", [o_l], [0], [], w_out, *out_args_l)
    x_c = None
    if need_ctx:
        o_c = gqa_attention(proj_c, ctx_tabs, [(proj_c, ctx_tabs)], q_gain, k_gain, n_kv)
        x_c = outproj("plain", [o_c.reshape(1, -1, o_c.shape[2])], [0], [], w_out, *out_args_c)
    return x_l, x_c


def _mixer_hgrn(proj_l, proj_c, w_out, lb, out_g, need_ctx, out_args_l, out_args_c):
    b, _, w = proj_l.shape
    d = w // HGRN_N_PROJ
    nh = d // HEAD_128
    lb3 = lb.reshape(2, 1, d)
    s0 = jnp.zeros((b, nh, HEAD_128, HEAD_128), F32)
    o_l, o_c = [], []
    for direction in range(2):
        oc, s_ctx = hgrn_scan(proj_c, lb3, s0, direction)
        ol, _ = hgrn_scan(proj_l, lb3, s_ctx, direction)
        o_l.append(ol)
        o_c.append(oc)
    og = jnp.tile(out_g, nh).reshape(1, d)
    g_col0 = 4 * d
    x_l = outproj("hgrn", [o_l[0], o_l[1], proj_l], [0, 0, g_col0], [og], w_out, *out_args_l)
    x_c = None
    if need_ctx:
        flat = lambda t: t.reshape(1, -1, t.shape[2])
        x_c = outproj("hgrn", [flat(o_c[0]), flat(o_c[1]), flat(proj_c)], [0, 0, g_col0], [og], w_out,
                      *out_args_c)
    return x_l, x_c


def _mixer_ssd(proj_l, proj_c, w_out, conv_w, conv_b, dt_bias, a_log, d_skip, norm_g, need_ctx,
               out_args_l, out_args_c):
    b = proj_l.shape[0]
    d_inner = w_out.shape[0]
    n_heads = d_inner // SSD_HEAD_DIM
    hpg = n_heads // SSD_GROUPS
    gn = SSD_GROUPS * SSD_STATE
    tc = 512
    x_blk0 = d_inner // tc
    dt_blk = (2 * d_inner + 2 * gn) // LANES
    cw_x, cw_bc = conv_w[:, :d_inner], conv_w[:, d_inner:]
    cb_x, cb_bc = conv_b[:d_inner].reshape(1, -1), conv_b[d_inner:].reshape(1, -1)
    dtb = dt_bias.reshape(1, 2 * n_heads)
    alog = a_log.reshape(1, 2 * n_heads)
    h0 = jnp.zeros((b, SSD_GROUPS, SSD_STATE, hpg * SSD_HEAD_DIM), F32)

    def prep(proj):
        xs = conv_silu(proj, x_blk0, d_inner, cw_x, cb_x, tc)
        bc = conv_silu(proj, x_blk0 + d_inner // tc, 2 * gn, cw_bc, cb_bc, tc)
        return xs, bc

    xs_l, bc_l = prep(proj_l)
    xs_c, bc_c = prep(proj_c)
    dsk = jnp.repeat(d_skip, SSD_HEAD_DIM).reshape(1, d_inner)
    y_l, y_c = None, None
    for direction in range(2):
        y_c, h_ctx = ssd_scan(xs_c, bc_c, proj_c, dt_blk, dtb, alog, dsk, h0, y_c, direction)
        y_l, _ = ssd_scan(xs_l, bc_l, proj_l, dt_blk, dtb, alog, dsk, h_ctx, y_l, direction)
    ng = norm_g.reshape(1, d_inner)
    ngrp = d_inner // SSD_GROUPS
    x_l = outproj("ssd", [y_l, proj_l], [0, 0], [ng], w_out, *out_args_l, norm_group=ngrp)
    x_c = None
    if need_ctx:
        flat = lambda t: t.reshape(1, -1, t.shape[2])
        x_c = outproj("ssd", [flat(y_c), flat(proj_c)], [0, 0], [ng], w_out, *out_args_c, norm_group=ngrp)
    return x_l, x_c


def _mixer_diff(proj_l, proj_c, w_out, lam_p, subln_g, lambda_init, out_args_l):
    n = proj_l.shape[1]
    c = proj_c.shape[1]
    lat_tabs = _rope_tables(n, LANES // 2, 2)
    ctx_tabs = _identity_tables(c)
    o_l = diff_attention(proj_l, lat_tabs, [(proj_l, lat_tabs), (proj_c, ctx_tabs)], lam_p,
                         subln_g.reshape(1, LANES), lambda_init)
    return outproj("plain", [o_l], [0], [], w_out, *out_args_l)


def kernel(x, c, ctx, c_ctx, w_mod, b_mod, norm_g, ffn_w13, ffn_w2, attn_w_in, attn_q_g, attn_k_g, attn_w_out, hgrn_w_in, hgrn_lb_logits, hgrn_out_g, hgrn_w_out, ssd_w_in, ssd_conv_w, ssd_conv_b, ssd_dt_bias, ssd_a_log, ssd_d, ssd_norm_g, ssd_w_out, diff_w_in, diff_lambda, diff_subln_g, diff_w_out):
    b, n, d = x.shape
    n_ctx = ctx.shape[1]
    depth = w_mod.shape[0]
    d_ff = ffn_w2.shape[1]
    n_mixers = 4

    n_rows = -(-(b + 1) // 16) * 16
    c_all = jnp.zeros((n_rows, d), F32).at[:b].set(c).at[b].set(c_ctx)
    mods = adaln_all(c_all, w_mod, b_mod).reshape(depth, n_rows, N_MOD, d)

    x_lat = x
    x_ctx = ctx.reshape(1, b * n_ctx, d)
    for layer in range(depth):
        kind, j = layer % n_mixers, layer // n_mixers
        need_ctx = layer < depth - 1
        m3 = mods[layer]
        g = norm_g[layer]
        g_rows = [g[i].reshape(1, d) for i in range(4)]

        if kind == 0:
            perm = _deinterleave_perm(attn_w_in.shape[2], HEAD_128)
            n_qk = (attn_w_in.shape[2] // HEAD_128 // (GQA_GROUP + 2)) * (GQA_GROUP + 1) * HEAD_128
            perm = jnp.where(jnp.arange(perm.shape[0]) < n_qk, perm, jnp.arange(perm.shape[0]))
            w_in = attn_w_in[j][:, perm]
        elif kind == 1:
            w_in = hgrn_w_in[j]
        elif kind == 2:
            w_in = ssd_w_in[j]
        else:
            perm = _deinterleave_perm(diff_w_in.shape[2], LANES // 2)
            n_qk = 2 * (diff_w_in.shape[2] // 3)
            perm = jnp.where(jnp.arange(perm.shape[0]) < n_qk, perm, jnp.arange(perm.shape[0]))
            w_in = diff_w_in[j][:, perm]
        w_in = w_in.astype(BF16)
        n_proj = w_in.shape[1]
        tn = 1152 if n_proj % 1024 else 1024

        proj_l = inproj(x_lat, g_rows[0], m3, 0, 0, w_in, n_proj, tn, F32)
        proj_c = inproj(x_ctx, g_rows[0], m3, b, 0, w_in, n_proj, tn, F32).reshape(b, n_ctx, n_proj)
        out_args_l = (x_lat, g_rows[1], m3, 0, 2)
        out_args_c = (x_ctx, g_rows[1], m3, b, 2)

        if kind == 0:
            x_lat, x_ctx_new = _mixer_gqa(proj_l, proj_c, attn_w_out[j].astype(BF16), attn_q_g[j], attn_k_g[j],
                                          need_ctx, out_args_l, out_args_c)
        elif kind == 1:
            cum = jnp.cumsum(jax.nn.softmax(hgrn_lb_logits.astype(F32), axis=1), axis=1)
            lb = cum[:, layer] - cum[:, 0]
            x_lat, x_ctx_new = _mixer_hgrn(proj_l, proj_c, hgrn_w_out[j].astype(BF16), lb, hgrn_out_g[j],
                                           need_ctx, out_args_l, out_args_c)
        elif kind == 2:
            x_lat, x_ctx_new = _mixer_ssd(proj_l, proj_c, ssd_w_out[j].astype(BF16), ssd_conv_w[j], ssd_conv_b[j],
                                          ssd_dt_bias[j], ssd_a_log[j], ssd_d[j], ssd_norm_g[j], need_ctx,
                                          out_args_l, out_args_c)
        else:
            lambda_init = 0.8 - 0.6 * math.exp(-0.3 * layer)
            x_lat = _mixer_diff(proj_l, proj_c, diff_w_out[j].astype(BF16), diff_lambda[j], diff_subln_g[j],
                                lambda_init, out_args_l)
            x_ctx_new = None

        w13 = ffn_w13[layer].astype(BF16)
        w2 = ffn_w2[layer].astype(BF16)
        u_l = inproj(x_lat, g_rows[2], m3, 0, 3, w13, d_ff, 512, BF16, swiglu=True)
        x_lat = outproj("plain", [u_l], [0], [], w2, x_lat, g_rows[3], m3, 0, 5)
        if need_ctx:
            x_ctx = x_ctx_new
            u_c = inproj(x_ctx, g_rows[2], m3, b, 3, w13, d_ff, 512, BF16, swiglu=True)
            x_ctx = outproj("plain", [u_c], [0], [], w2, x_ctx, g_rows[3], m3, b, 5)
    return x_lat
```

```python
import functools
import math

import jax
import jax.numpy as jnp
from jax import lax
from jax.experimental import pallas as pl
from jax.experimental.pallas import tpu as pltpu

F32 = jnp.float32
BF16 = jnp.bfloat16

NORM_EPS = 1e-6
LOG2_E = math.log2(math.e)
ROPE_THETA = 10000.0
GRID_W = 64
N_MOD = 6
HEAD_128 = 128
GQA_GROUP = 4
HGRN_CHUNK = 32
HGRN_N_PROJ = 5
SSD_HEAD_DIM = 64
SSD_GROUPS = 8
SSD_STATE = 128
DIFF_HEADS = 16

LANES = 128
VMEM_LIMIT_BYTES = 56 * 2**20
TM_IN = 1024
TM_OUT = 512
ROW_CHUNK = 16
RESIDENT_K_FULL_TILE = 4096
READOUT_SPLIT = 2
ROW_UNROLL = 8
TQ_GQA = 512
TQ_DIFF = 1024
HGRN_BLOCK = 256
HGRN_HEADS_PER_STEP = 8
SSD_CHUNK = 128


def _cparams(n_axes):
    return pltpu.CompilerParams(dimension_semantics=("arbitrary",) * n_axes,
                                vmem_limit_bytes=VMEM_LIMIT_BYTES)


def _silu(x):
    return x / (1.0 + jnp.exp(-x))


def _dot(a, b):
    return jnp.dot(a, b, preferred_element_type=F32)


def _dot_nt(a, b):
    return lax.dot_general(a, b, (((1,), (1,)), ((), ())), preferred_element_type=F32)


def _dot_tn(a, b):
    return lax.dot_general(a, b, (((0,), (0,)), ((), ())), preferred_element_type=F32)


def _dot_exact_lhs(m01, x):
    hi = x.astype(BF16)
    r1 = x - hi.astype(F32)
    mid = r1.astype(BF16)
    lo = (r1 - mid.astype(F32)).astype(BF16)
    return _dot(m01, hi) + _dot(m01, mid) + _dot(m01, lo)


def _adaln_kernel(c_ref, w_ref, b_ref, o_ref):
    s = _silu(c_ref[...]).astype(BF16)
    o_ref[0] = _dot(s, w_ref[0].astype(BF16)) + b_ref[0]


def adaln_all(c_all, w_mod, b_mod):
    n_layers, d, n = w_mod.shape
    r = c_all.shape[0]
    tn = 1024
    return pl.pallas_call(
        _adaln_kernel,
        grid=(n_layers, n // tn),
        in_specs=[pl.BlockSpec((r, d), lambda l, j: (0, 0)),
                  pl.BlockSpec((1, d, tn), lambda l, j: (l, 0, j)),
                  pl.BlockSpec((1, 1, tn), lambda l, j: (l, 0, j))],
        out_specs=pl.BlockSpec((1, r, tn), lambda l, j: (l, 0, j)),
        out_shape=jax.ShapeDtypeStruct((n_layers, r, n), F32),
        compiler_params=_cparams(2),
        name="adaln",
    )(c_all, w_mod, b_mod.reshape(n_layers, 1, n))


def _inproj_kernel(x_ref, g_ref, m_ref, *rest, shift_row, n_w, swiglu):
    w_refs = rest[:n_w]
    o_ref = rest[n_w]
    h_ref = rest[n_w + 1]

    @pl.when(pl.program_id(2) == 0)
    def _():
        shift = m_ref[0, shift_row:shift_row + 1, :]
        gain = g_ref[...] * (1.0 + m_ref[0, shift_row + 1:shift_row + 2, :])

        def rows(i, carry):
            r0 = pl.multiple_of(i * ROW_CHUNK, ROW_CHUNK)
            x = x_ref[0, pl.ds(r0, ROW_CHUNK), :]
            ms = jnp.mean(x * x, axis=-1, keepdims=True)
            h_ref[pl.ds(r0, ROW_CHUNK), :] = (x * lax.rsqrt(ms + NORM_EPS) * gain + shift).astype(BF16)
            return carry

        lax.fori_loop(0, h_ref.shape[0] // ROW_CHUNK, rows, 0, unroll=ROW_UNROLL)

    h = h_ref[...]
    if swiglu:
        half = o_ref.shape[2] // 2
        for c in range(2):
            cs = slice(c * half, (c + 1) * half)
            gate = _dot(h, w_refs[0][:, cs])
            up = _dot(h, w_refs[1][:, cs])
            o_ref[0, :, cs] = (_silu(gate) * up).astype(o_ref.dtype)
    else:
        o_ref[0] = _dot(h, w_refs[0][...]).astype(o_ref.dtype)


def inproj(x3, g_row, mods3, mod_row0, shift_row, w, n_out, tn, out_dtype, swiglu=False):
    bx, s, d = x3.shape
    tm = min(TM_IN, s)
    offs = (0, n_out // tn) if swiglu else (0,)
    in_specs = [pl.BlockSpec((1, tm, d), lambda b, i, j: (b, i, 0)),
                pl.BlockSpec((1, d), lambda b, i, j: (0, 0)),
                pl.BlockSpec((1, N_MOD, d), lambda b, i, j: (b + mod_row0, 0, 0))]
    for off in offs:
        in_specs.append(pl.BlockSpec((d, tn), lambda b, i, j, off=off: (0, j + off)))
    return pl.pallas_call(
        functools.partial(_inproj_kernel, shift_row=shift_row, n_w=len(offs), swiglu=swiglu),
        grid=(bx, s // tm, n_out // tn),
        in_specs=in_specs,
        out_specs=pl.BlockSpec((1, tm, tn), lambda b, i, j: (b, i, j)),
        out_shape=jax.ShapeDtypeStruct((bx, s, n_out), out_dtype),
        scratch_shapes=[pltpu.VMEM((tm, d), BF16)],
        compiler_params=_cparams(3),
        name="inproj_swiglu" if swiglu else "inproj",
    )(x3, g_row, mods3, *([w] * len(offs)))


def _outproj_kernel(*refs, mode, gate_row, norm_group):
    n_in = {"plain": 1, "hgrn": 4, "ssd": 3}[mode]
    ins = refs[:n_in]
    w_ref, res_ref, g_ref, m_ref, o_ref, acc_ref = refs[n_in:]

    def readout(rs):
        if mode == "plain":
            return ins[0][0, rs, :]
        if mode == "hgrn":
            of_ref, ob_ref, gp_ref, og_ref = ins
            o = of_ref[0, rs, :] + ob_ref[0, rs, :]
            pieces = []
            for h in range(o.shape[-1] // HEAD_128):
                sl = slice(h * HEAD_128, (h + 1) * HEAD_128)
                oh = o[:, sl]
                ms = jnp.mean(oh * oh, axis=-1, keepdims=True)
                pieces.append(oh * lax.rsqrt(ms + NORM_EPS) * og_ref[:, sl])
            return (jnp.concatenate(pieces, axis=-1) * _silu(gp_ref[0, rs, :])).astype(BF16)
        y_ref, z_ref, ng_ref = ins
        pieces = []
        for gi in range(y_ref.shape[2] // norm_group):
            sl = slice(gi * norm_group, (gi + 1) * norm_group)
            y = y_ref[0, rs, sl] * _silu(z_ref[0, rs, sl])
            ms = jnp.mean(y * y, axis=-1, keepdims=True)
            pieces.append((y * lax.rsqrt(ms + NORM_EPS) * ng_ref[:, sl]).astype(BF16))
        return pieces[0] if len(pieces) == 1 else jnp.concatenate(pieces, axis=-1)

    tm = acc_ref.shape[0]
    n_split = 1 if mode == "plain" else READOUT_SPLIT
    for s in range(n_split):
        rs = slice(s * tm // n_split, (s + 1) * tm // n_split)
        acc_ref[rs, :] = _dot(readout(rs), w_ref[...])

    gain = g_ref[...] * m_ref[0, gate_row:gate_row + 1, :]

    def rows(i, carry):
        r0 = pl.multiple_of(i * ROW_CHUNK, ROW_CHUNK)
        y = acc_ref[pl.ds(r0, ROW_CHUNK), :]
        ms = jnp.mean(y * y, axis=-1, keepdims=True)
        o_ref[0, pl.ds(r0, ROW_CHUNK), :] = (res_ref[0, pl.ds(r0, ROW_CHUNK), :]
                                             + y * lax.rsqrt(ms + NORM_EPS) * gain)
        return carry

    lax.fori_loop(0, tm // ROW_CHUNK, rows, 0, unroll=ROW_UNROLL)


def outproj(mode, ins, in_cols, row_ins, w, res3, g_row, mods3, mod_row0, gate_row, norm_group=0):
    bx, s, d = res3.shape
    kdim = w.shape[0]
    tm = min(TM_OUT if (mode == "plain" and kdim <= RESIDENT_K_FULL_TILE) else TM_OUT // 2, s)
    in_specs = [pl.BlockSpec((1, tm, kdim), lambda b, i, off=c // kdim: (b, i, off)) for c in in_cols]
    in_specs += [pl.BlockSpec((1, kdim), lambda b, i: (0, 0)) for _ in row_ins]
    in_specs += [pl.BlockSpec((kdim, d), lambda b, i: (0, 0), pipeline_mode=pl.Buffered(1)),
                 pl.BlockSpec((1, tm, d), lambda b, i: (b, i, 0)),
                 pl.BlockSpec((1, d), lambda b, i: (0, 0)),
                 pl.BlockSpec((1, N_MOD, d), lambda b, i: (b + mod_row0, 0, 0))]
    return pl.pallas_call(
        functools.partial(_outproj_kernel, mode=mode, gate_row=gate_row, norm_group=norm_group),
        grid=(bx, s // tm),
        in_specs=in_specs,
        out_specs=pl.BlockSpec((1, tm, d), lambda b, i: (b, i, 0)),
        out_shape=jax.ShapeDtypeStruct((bx, s, d), F32),
        scratch_shapes=[pltpu.VMEM((tm, d), F32)],
        compiler_params=_cparams(2),
        name="outproj_" + mode,
    )(*ins, *row_ins, w, res3, g_row, mods3)


def _deinterleave_perm(width, block):
    idx = jnp.arange(width).reshape(width // block, block)
    return jnp.concatenate([idx[:, 0::2], idx[:, 1::2]], axis=1).reshape(width)


def _rope_tables(n, head_dim, reps):
    rows = n // GRID_W
    row = jnp.repeat(jnp.arange(rows, dtype=F32), GRID_W)
    col = jnp.tile(jnp.arange(GRID_W, dtype=F32), rows)
    axis_dim = head_dim // 2
    inv_freq = ROPE_THETA ** (-jnp.arange(0, axis_dim, 2, dtype=F32) / axis_dim)
    ang = jnp.concatenate([row[:, None] * inv_freq, col[:, None] * inv_freq], axis=-1)
    cos, sin = jnp.cos(ang), jnp.sin(ang)
    cos_t = jnp.tile(jnp.concatenate([cos, cos], axis=-1), (1, reps))
    sin_t = jnp.tile(jnp.concatenate([-sin, sin], axis=-1), (1, reps))
    return cos_t, sin_t


def _identity_tables(n):
    return jnp.ones((n, LANES), F32), jnp.zeros((n, LANES), F32)


def _rot_half_128(y):
    return pltpu.roll(y, 64, 1)


def _rot_half_64(y):
    lane = lax.broadcasted_iota(jnp.int32, y.shape, 1)
    first = jnp.bitwise_and(lane, 63) < 32
    return jnp.where(first, pltpu.roll(y, 96, 1), pltpu.roll(y, 32, 1))


def _col_reduce(x, op):
    n, t = x.shape
    if n > LANES:
        x = op(x.reshape(n // LANES, LANES, t), axis=0)
    return op(x, axis=0, keepdims=True)


def _softmax_numerators_t(scores):
    m = functools.reduce(jnp.maximum, [_col_reduce(sc, jnp.max) for sc in scores])
    ps = [jnp.exp2(sc - m) for sc in scores]
    l = functools.reduce(jnp.add, [_col_reduce(p, jnp.sum) for p in ps])
    return ps, l


def _softmax_pv_t(scores, v_t):
    ps, l = _softmax_numerators_t(scores)
    o_t = functools.reduce(jnp.add, [_dot(v, p.astype(BF16)) for v, p in zip(v_t, ps)])
    return o_t, l


def _gqa_norm_rope(x, gain, cos, sin):
    ms = jnp.mean(x * x, axis=-1, keepdims=True)
    y = x * lax.rsqrt(ms + NORM_EPS) * gain
    return y * cos + _rot_half_128(y) * sin


def _gqa_kernel(q_ref, cq_ref, sq_ref, qg_ref, kg_ref, *rest, n_seg, group, scale):
    seg_refs = rest[:4 * n_seg]
    o_ref = rest[4 * n_seg]
    scr = rest[4 * n_seg + 1:]

    @pl.when(pl.program_id(2) == 0)
    def _():
        for s in range(n_seg):
            k_ref, v_ref, ck_ref, sk_ref = seg_refs[4 * s:4 * s + 4]
            scr[2 * s][...] = _gqa_norm_rope(k_ref[0], kg_ref[...], ck_ref[...], sk_ref[...]).astype(BF16)
            scr[2 * s + 1][...] = v_ref[0].T.astype(BF16)

    heads = [slice(g * HEAD_128, (g + 1) * HEAD_128) for g in range(group)]
    qn = [(_gqa_norm_rope(q_ref[0, :, sl], qg_ref[...], cq_ref[...], sq_ref[...]) * (scale * LOG2_E)).astype(BF16)
          for sl in heads]
    def qk(g):
        return [_dot_nt(scr[2 * s][...], qn[g]) for s in range(n_seg)]

    scores = qk(0)
    for g, sl in enumerate(heads):
        nxt = qk(g + 1) if g + 1 < group else None
        o_t, l = _softmax_pv_t(scores, [scr[2 * s + 1][...] for s in range(n_seg)])
        o_ref[0, :, sl] = (o_t * (1.0 / l)).T.astype(o_ref.dtype)
        scores = nxt


def gqa_attention(q_src, q_tabs, segs, q_gain, k_gain, n_kv):
    b, sq, _ = q_src.shape
    group = GQA_GROUP
    tq = min(TQ_GQA, sq)
    qw = group * HEAD_128
    k_blk0 = n_kv * group
    v_blk0 = k_blk0 + n_kv
    in_specs = [pl.BlockSpec((1, tq, qw), lambda bi, h, i: (bi, i, h)),
                pl.BlockSpec((tq, LANES), lambda bi, h, i: (i, 0)),
                pl.BlockSpec((tq, LANES), lambda bi, h, i: (i, 0)),
                pl.BlockSpec((1, LANES), lambda bi, h, i: (0, 0)),
                pl.BlockSpec((1, LANES), lambda bi, h, i: (0, 0))]
    args = [q_src, q_tabs[0], q_tabs[1], q_gain, k_gain]
    scratch = []
    for kv_src, (ck, sk) in segs:
        ks = kv_src.shape[1]
        in_specs += [pl.BlockSpec((1, ks, LANES), lambda bi, h, i: (bi, 0, k_blk0 + h)),
                     pl.BlockSpec((1, ks, LANES), lambda bi, h, i: (bi, 0, v_blk0 + h)),
                     pl.BlockSpec((ks, LANES), lambda bi, h, i: (0, 0)),
                     pl.BlockSpec((ks, LANES), lambda bi, h, i: (0, 0))]
        args += [kv_src, kv_src, ck, sk]
        scratch += [pltpu.VMEM((ks, LANES), BF16), pltpu.VMEM((LANES, ks), BF16)]
    return pl.pallas_call(
        functools.partial(_gqa_kernel, n_seg=len(segs), group=group, scale=HEAD_128 ** -0.5),
        grid=(b, n_kv, sq // tq),
        in_specs=in_specs,
        out_specs=pl.BlockSpec((1, tq, qw), lambda bi, h, i: (bi, i, h)),
        out_shape=jax.ShapeDtypeStruct((b, sq, n_kv * qw), BF16),
        scratch_shapes=scratch,
        compiler_params=_cparams(3),
        name="gqa_attention",
    )(*args)


def _diff_kernel(q_ref, cq_ref, sq_ref, lam_ref, sg_ref, *rest, n_seg, scale, lambda_init):
    seg_refs = rest[:4 * n_seg]
    o_ref = rest[4 * n_seg]
    scr = rest[4 * n_seg + 1:]

    @pl.when(pl.program_id(2) == 0)
    def _():
        for s in range(n_seg):
            k_ref, v_ref, ck_ref, sk_ref = seg_refs[4 * s:4 * s + 4]
            k = k_ref[0]
            scr[2 * s][...] = (k * ck_ref[...] + _rot_half_64(k) * sk_ref[...]).astype(BF16)
            scr[2 * s + 1][...] = v_ref[0].T.astype(BF16)

    q = q_ref[0]
    q = (q * cq_ref[...] + _rot_half_64(q) * sq_ref[...]) * (scale * LOG2_E)
    lane = lax.broadcasted_iota(jnp.int32, q.shape, 1)
    zero = jnp.zeros_like(q)
    qs = [jnp.where(lane < 64, q, zero).astype(BF16), jnp.where(lane < 64, zero, q).astype(BF16)]

    lp = lam_ref[...]
    lam = (jnp.exp(jnp.sum(lp[0:1] * lp[1:2], axis=-1, keepdims=True))
           - jnp.exp(jnp.sum(lp[2:3] * lp[3:4], axis=-1, keepdims=True)) + lambda_init)

    scores = [[_dot_nt(scr[2 * s][...], qc) for s in range(n_seg)] for qc in qs]
    (p1, l1), (p2, l2) = [_softmax_numerators_t(sc) for sc in scores]
    r = lam * l1 / l2
    o_t = functools.reduce(jnp.add, [
        _dot(scr[2 * s + 1][...], (p1[s] - r * p2[s]).astype(BF16)) for s in range(n_seg)])
    o = (o_t * (1.0 / l1)).T
    ms = jnp.mean(o * o, axis=-1, keepdims=True)
    o_ref[0] = (o * lax.rsqrt(ms + NORM_EPS) * sg_ref[...] * (1.0 - lambda_init)).astype(o_ref.dtype)


def diff_attention(q_src, q_tabs, segs, lam_p, subln_g, lambda_init):
    b, sq, _ = q_src.shape
    nh = DIFF_HEADS
    tq = min(TQ_DIFF, sq)
    in_specs = [pl.BlockSpec((1, tq, LANES), lambda bi, h, i: (bi, i, h)),
                pl.BlockSpec((tq, LANES), lambda bi, h, i: (i, 0)),
                pl.BlockSpec((tq, LANES), lambda bi, h, i: (i, 0)),
                pl.BlockSpec(lam_p.shape, lambda bi, h, i: (0, 0)),
                pl.BlockSpec((1, LANES), lambda bi, h, i: (0, 0))]
    args = [q_src, q_tabs[0], q_tabs[1], lam_p, subln_g]
    scratch = []
    for kv_src, (ck, sk) in segs:
        ks = kv_src.shape[1]
        in_specs += [pl.BlockSpec((1, ks, LANES), lambda bi, h, i: (bi, 0, nh + h)),
                     pl.BlockSpec((1, ks, LANES), lambda bi, h, i: (bi, 0, 2 * nh + h)),
                     pl.BlockSpec((ks, LANES), lambda bi, h, i: (0, 0)),
                     pl.BlockSpec((ks, LANES), lambda bi, h, i: (0, 0))]
        args += [kv_src, kv_src, ck, sk]
        scratch += [pltpu.VMEM((ks, LANES), BF16), pltpu.VMEM((LANES, ks), BF16)]
    return pl.pallas_call(
        functools.partial(_diff_kernel, n_seg=len(segs), scale=(LANES // 2) ** -0.5, lambda_init=lambda_init),
        grid=(b, nh, sq // tq),
        in_specs=in_specs,
        out_specs=pl.BlockSpec((1, tq, LANES), lambda bi, h, i: (bi, i, h)),
        out_shape=jax.ShapeDtypeStruct((b, sq, nh * LANES), BF16),
        scratch_shapes=scratch,
        compiler_params=_cparams(3),
        name="diff_attention",
    )(*args)


def _hgrn_kernel(q_ref, f_ref, v_ref, lb_ref, s0_ref, o_ref, sout_ref, st_ref, *, reverse, hpb, blk_len, nblk):
    blk = pl.program_id(2)

    @pl.when(blk == 0)
    def _():
        st_ref[...] = s0_ref[0]

    n_chunks = blk_len // HGRN_CHUNK
    row = lax.broadcasted_iota(jnp.int32, (blk_len, blk_len), 0)
    col = lax.broadcasted_iota(jnp.int32, (blk_len, blk_len), 1)
    same = lax.shift_right_logical(row, 5) == lax.shift_right_logical(col, 5)
    tri = jnp.logical_and(same, (col >= row) if reverse else (col <= row))
    tri01 = tri.astype(F32).astype(BF16)
    heads = [slice(h * HEAD_128, (h + 1) * HEAD_128) for h in range(hpb)]
    chunks = [slice(c * HGRN_CHUNK, (c + 1) * HGRN_CHUNK) for c in range(n_chunks)]

    q = _silu(q_ref[0])
    lb = lb_ref[0]
    forget = lb + (1.0 - lb) * (1.0 / (1.0 + jnp.exp(-f_ref[0])))
    k = 1.0 - forget
    cum = _dot_exact_lhs(tri01, jnp.log(forget))
    last_rows = [cum[c.start:c.start + 1] if reverse else cum[c.stop - 1:c.stop] for c in chunks]
    last = jnp.concatenate([jnp.broadcast_to(r, (HGRN_CHUNK, r.shape[1])) for r in last_rows], axis=0)
    q_dec = (q * jnp.exp(cum)).astype(BF16)
    k_inv = (k * jnp.exp(-cum)).astype(BF16)
    k_end = (k * jnp.exp(last - cum)).astype(BF16)
    vb = v_ref[0].astype(BF16)
    e_last = [jnp.exp(r) for r in last_rows]
    att = [jnp.where(tri, _dot_nt(q_dec[:, sl], k_inv[:, sl]), 0.0).astype(BF16) for sl in heads]
    o_intra = [_dot(att[h], vb[:, sl]) for h, sl in enumerate(heads)]
    upd = [[_dot_tn(vb[rs, sl], k_end[rs, sl]) for sl in heads] for rs in chunks]
    st = [st_ref[h] for h in range(hpb)]
    for ci in range(n_chunks):
        c = n_chunks - 1 - ci if reverse else ci
        rs = chunks[c]
        for h, sl in enumerate(heads):
            o_ref[0, rs, sl] = o_intra[h][rs] + _dot_nt(q_dec[rs, sl], st[h].astype(BF16))
            st[h] = st[h] * e_last[c][:, sl] + upd[c][h]
    for h in range(hpb):
        st_ref[h] = st[h]

    @pl.when(blk == nblk - 1)
    def _():
        sout_ref[0] = st_ref[...]


def hgrn_scan(proj, lb3, s0, direction):
    b, s, w = proj.shape
    d = w // HGRN_N_PROJ
    nh = d // HEAD_128
    hpb = HGRN_HEADS_PER_STEP
    hw = hpb * HEAD_128
    blk_len = min(HGRN_BLOCK, s)
    nblk = s // blk_len
    ncb = d // hw
    reverse = direction == 1

    def tok(j):
        return nblk - 1 - j if reverse else j

    in_specs = [pl.BlockSpec((1, blk_len, hw), lambda bi, h, j: (bi, tok(j), h)),
                pl.BlockSpec((1, blk_len, hw), lambda bi, h, j: (bi, tok(j), (1 + direction) * ncb + h)),
                pl.BlockSpec((1, blk_len, hw), lambda bi, h, j: (bi, tok(j), 3 * ncb + h)),
                pl.BlockSpec((1, 1, hw), lambda bi, h, j: (direction, 0, h)),
                pl.BlockSpec((1, hpb, HEAD_128, HEAD_128), lambda bi, h, j: (bi, h, 0, 0))]
    return pl.pallas_call(
        functools.partial(_hgrn_kernel, reverse=reverse, hpb=hpb, blk_len=blk_len, nblk=nblk),
        grid=(b, nh // hpb, nblk),
        in_specs=in_specs,
        out_specs=[pl.BlockSpec((1, blk_len, hw), lambda bi, h, j: (bi, tok(j), h)),
                   pl.BlockSpec((1, hpb, HEAD_128, HEAD_128), lambda bi, h, j: (bi, h, 0, 0))],
        out_shape=[jax.ShapeDtypeStruct((b, s, d), F32),
                   jax.ShapeDtypeStruct((b, nh, HEAD_128, HEAD_128), F32)],
        scratch_shapes=[pltpu.VMEM((hpb, HEAD_128, HEAD_128), F32)],
        compiler_params=_cparams(3),
        name="hgrn_scan",
    )(proj, proj, proj, lb3, s0)


def _conv_silu_kernel(x_ref, w_ref, b_ref, o_ref):
    u = x_ref[0]
    n = u.shape[0]
    row = lax.broadcasted_iota(jnp.int32, u.shape, 0)
    prev = jnp.where(row == 0, 0.0, pltpu.roll(u, 1, 0))
    nxt = jnp.where(row == n - 1, 0.0, pltpu.roll(u, n - 1, 0))
    y = prev * w_ref[0:1, :] + u * w_ref[1:2, :] + nxt * w_ref[2:3, :] + b_ref[...]
    o_ref[0] = _silu(y).astype(o_ref.dtype)


def conv_silu(proj, col_blk0, width, conv_w, conv_b, tc=512):
    b, s, _ = proj.shape
    return pl.pallas_call(
        _conv_silu_kernel,
        grid=(b, width // tc),
        in_specs=[pl.BlockSpec((1, s, tc), lambda bi, j: (bi, 0, col_blk0 + j)),
                  pl.BlockSpec((3, tc), lambda bi, j: (0, j)),
                  pl.BlockSpec((1, tc), lambda bi, j: (0, j))],
        out_specs=pl.BlockSpec((1, s, tc), lambda bi, j: (bi, 0, j)),
        out_shape=jax.ShapeDtypeStruct((b, s, width), BF16),
        compiler_params=_cparams(2),
        name="ssd_conv_silu",
    )(proj, conv_w, conv_b)


def _ssd_kernel(x_ref, bc_ref, dt_ref, dtb_ref, alog_ref, dsk_ref, h0_ref, *rest,
                reverse, direction, nck, n_groups, hpg, has_prev):
    yprev_ref = rest[0] if has_prev else None
    y_ref, hout_ref, st_ref = rest[-3:]
    ck = pl.program_id(1)
    lc = SSD_CHUNK
    gw = hpg * SSD_HEAD_DIM
    n_heads = n_groups * hpg

    @pl.when(ck == 0)
    def _():
        st_ref[...] = h0_ref[0]

    row = lax.broadcasted_iota(jnp.int32, (lc, lc), 0)
    col = lax.broadcasted_iota(jnp.int32, (lc, lc), 1)
    tri = (col >= row) if reverse else (col <= row)
    tri01 = tri.astype(F32).astype(BF16)
    lo = lax.broadcasted_iota(jnp.int32, (lc, LANES), 1) < SSD_HEAD_DIM
    lo_row = lo[0:1, :]

    x_in = dt_ref[0] + dtb_ref[...]
    dt = jnp.maximum(x_in, 0.0) + jnp.log1p(jnp.exp(-jnp.abs(x_in)))
    a = -jnp.exp(alog_ref[...])
    cum = _dot_exact_lhs(tri01, dt * a)
    cum_t = cum.T
    dt_t = dt.T
    t_last = 0 if reverse else lc - 1

    for g in range(n_groups):
        b_g = bc_ref[0, :, g * SSD_STATE:(g + 1) * SSD_STATE]
        c_g = bc_ref[0, :, (n_groups + g) * SSD_STATE:(n_groups + g + 1) * SSD_STATE]
        b_t = b_g.astype(F32).T
        cb = _dot_nt(c_g, b_g)
        h_g = st_ref[g]
        y_state = _dot(c_g, h_g.astype(BF16))
        for p in range(hpg // 2):
            cols = slice(g * gw + p * LANES, g * gw + (p + 1) * LANES)
            x2 = x_ref[0, :, cols].astype(F32)
            zero = jnp.zeros_like(x2)
            x_halves = [jnp.where(lo, x2, zero).astype(BF16), jnp.where(lo, zero, x2).astype(BF16)]
            y_parts, upd_parts, e_cols, decays = [], [], [], []
            for jj in range(2):
                r = direction * n_heads + g * hpg + 2 * p + jj
                row_b = jnp.broadcast_to(cum_t[r:r + 1, :], (lc, lc))
                col_b = row_b.T
                dt_row = dt_t[r:r + 1, :]
                decay = jnp.exp(jnp.where(tri, col_b - row_b, -jnp.inf))
                w = (cb * decay * dt_row).astype(BF16)
                y_parts.append(_dot(w, x_halves[jj]))
                last_b = col_b[t_last:t_last + 1, :]
                coef_row = jnp.exp(last_b - row_b[0:1, :]) * dt_row
                upd_parts.append(_dot((b_t * coef_row).astype(BF16), x_halves[jj]))
                e_cols.append(jnp.exp(col_b))
                decays.append(jnp.exp(last_b))
            cs = slice(p * LANES, (p + 1) * LANES)
            base = yprev_ref[0, :, cols] if has_prev else dsk_ref[:, cols] * x2
            y_ref[0, :, cols] = base + (
                y_parts[0] + y_parts[1] + y_state[:, cs] * jnp.where(lo, e_cols[0], e_cols[1]))
            st_ref[g, :, cs] = (h_g[:, cs] * jnp.where(lo_row, decays[0], decays[1])
                                + upd_parts[0] + upd_parts[1])

    @pl.when(ck == nck - 1)
    def _():
        hout_ref[0] = st_ref[...]


def ssd_scan(xs, bc, proj, dt_blk, dt_bias, a_log, d_skip, h0, y_prev, direction):
    b, s, d_inner = xs.shape
    has_prev = y_prev is not None
    n_groups = SSD_GROUPS
    hpg = d_inner // SSD_HEAD_DIM // n_groups
    gw = hpg * SSD_HEAD_DIM
    lc = SSD_CHUNK
    nck = s // lc
    reverse = direction == 1

    def tok(j):
        return nck - 1 - j if reverse else j

    seq_spec = pl.BlockSpec((1, lc, d_inner), lambda bi, j: (bi, tok(j), 0))
    return pl.pallas_call(
        functools.partial(_ssd_kernel, reverse=reverse, direction=direction, nck=nck, n_groups=n_groups, hpg=hpg,
                          has_prev=has_prev),
        grid=(b, nck),
        in_specs=[seq_spec,
                  pl.BlockSpec((1, lc, bc.shape[2]), lambda bi, j: (bi, tok(j), 0)),
                  pl.BlockSpec((1, lc, LANES), lambda bi, j: (bi, tok(j), dt_blk)),
                  pl.BlockSpec((1, LANES), lambda bi, j: (0, 0)),
                  pl.BlockSpec((1, LANES), lambda bi, j: (0, 0)),
                  pl.BlockSpec((1, d_inner), lambda bi, j: (0, 0)),
                  pl.BlockSpec((1, n_groups, SSD_STATE, gw), lambda bi, j: (bi, 0, 0, 0))]
        + ([seq_spec] if has_prev else []),
        out_specs=[pl.BlockSpec((1, lc, d_inner), lambda bi, j: (bi, tok(j), 0)),
                   pl.BlockSpec((1, n_groups, SSD_STATE, gw), lambda bi, j: (bi, 0, 0, 0))],
        out_shape=[jax.ShapeDtypeStruct((b, s, d_inner), F32),
                   jax.ShapeDtypeStruct((b, n_groups, SSD_STATE, gw), F32)],
        scratch_shapes=[pltpu.VMEM((n_groups, SSD_STATE, gw), F32)],
        compiler_params=_cparams(2),
        name="ssd_scan",
    )(xs, bc, proj, dt_bias, a_log, d_skip, h0, *([y_prev] if has_prev else []))


def _mixer_gqa(proj_l, proj_c, w_out, q_g, k_g, need_ctx, out_args_l, out_args_c):
    n = proj_l.shape[1]
    c = proj_c.shape[1]
    n_kv = proj_l.shape[2] // HEAD_128 // (GQA_GROUP + 2)
    perm = _deinterleave_perm(HEAD_128, HEAD_128)
    q_gain = q_g[perm].reshape(1, HEAD_128)
    k_gain = k_g[perm].reshape(1, HEAD_128)
    lat_tabs = _rope_tables(n, HEAD_128, 1)
    ctx_tabs = _identity_tables(c)
    o_l = gqa_attention(proj_l, lat_tabs, [(proj_l, lat_tabs), (proj_c, ctx_tabs)], q_gain, k_gain, n_kv)
    x_l = outproj("plain", [o_l], [0], [], w_out, *out_args_l)
    x_c = None
    if need_ctx:
        o_c = gqa_attention(proj_c, ctx_tabs, [(proj_c, ctx_tabs)], q_gain, k_gain, n_kv)
        x_c = outproj("plain", [o_c.reshape(1, -1, o_c.shape[2])], [0], [], w_out, *out_args_c)
    return x_l, x_c


def _mixer_hgrn(proj_l, proj_c, w_out, lb, out_g, need_ctx, out_args_l, out_args_c):
    b, _, w = proj_l.shape
    d = w // HGRN_N_PROJ
    nh = d // HEAD_128
    lb3 = lb.reshape(2, 1, d)
    s0 = jnp.zeros((b, nh, HEAD_128, HEAD_128), F32)
    o_l, o_c = [], []
    for direction in range(2):
        oc, s_ctx = hgrn_scan(proj_c, lb3, s0, direction)
        ol, _ = hgrn_scan(proj_l, lb3, s_ctx, direction)
        o_l.append(ol)
        o_c.append(oc)
    og = jnp.tile(out_g, nh).reshape(1, d)
    g_col0 = 4 * d
    x_l = outproj("hgrn", [o_l[0], o_l[1], proj_l], [0, 0, g_col0], [og], w_out, *out_args_l)
    x_c = None
    if need_ctx:
        flat = lambda t: t.reshape(1, -1, t.shape[2])
        x_c = outproj("hgrn", [flat(o_c[0]), flat(o_c[1]), flat(proj_c)], [0, 0, g_col0], [og], w_out,
                      *out_args_c)
    return x_l, x_c


def _mixer_ssd(proj_l, proj_c, w_out, conv_w, conv_b, dt_bias, a_log, d_skip, norm_g, need_ctx,
               out_args_l, out_args_c):
    b = proj_l.shape[0]
    d_inner = w_out.shape[0]
    n_heads = d_inner // SSD_HEAD_DIM
    hpg = n_heads // SSD_GROUPS
    gn = SSD_GROUPS * SSD_STATE
    tc = 512
    x_blk0 = d_inner // tc
    dt_blk = (2 * d_inner + 2 * gn) // LANES
    cw_x, cw_bc = conv_w[:, :d_inner], conv_w[:, d_inner:]
    cb_x, cb_bc = conv_b[:d_inner].reshape(1, -1), conv_b[d_inner:].reshape(1, -1)
    dtb = dt_bias.reshape(1, 2 * n_heads)
    alog = a_log.reshape(1, 2 * n_heads)
    h0 = jnp.zeros((b, SSD_GROUPS, SSD_STATE, hpg * SSD_HEAD_DIM), F32)

    def prep(proj):
        xs = conv_silu(proj, x_blk0, d_inner, cw_x, cb_x, tc)
        bc = conv_silu(proj, x_blk0 + d_inner // tc, 2 * gn, cw_bc, cb_bc, tc)
        return xs, bc

    xs_l, bc_l = prep(proj_l)
    xs_c, bc_c = prep(proj_c)
    dsk = jnp.repeat(d_skip, SSD_HEAD_DIM).reshape(1, d_inner)
    y_l, y_c = None, None
    for direction in range(2):
        y_c, h_ctx = ssd_scan(xs_c, bc_c, proj_c, dt_blk, dtb, alog, dsk, h0, y_c, direction)
        y_l, _ = ssd_scan(xs_l, bc_l, proj_l, dt_blk, dtb, alog, dsk, h_ctx, y_l, direction)
    ng = norm_g.reshape(1, d_inner)
    ngrp = d_inner // SSD_GROUPS
    x_l = outproj("ssd", [y_l, proj_l], [0, 0], [ng], w_out, *out_args_l, norm_group=ngrp)
    x_c = None
    if need_ctx:
        flat = lambda t: t.reshape(1, -1, t.shape[2])
        x_c = outproj("ssd", [flat(y_c), flat(proj_c)], [0, 0], [ng], w_out, *out_args_c, norm_group=ngrp)
    return x_l, x_c


def _mixer_diff(proj_l, proj_c, w_out, lam_p, subln_g, lambda_init, out_args_l):
    n = proj_l.shape[1]
    c = proj_c.shape[1]
    lat_tabs = _rope_tables(n, LANES // 2, 2)
    ctx_tabs = _identity_tables(c)
    o_l = diff_attention(proj_l, lat_tabs, [(proj_l, lat_tabs), (proj_c, ctx_tabs)], lam_p,
                         subln_g.reshape(1, LANES), lambda_init)
    return outproj("plain", [o_l], [0], [], w_out, *out_args_l)


def kernel(x, c, ctx, c_ctx, w_mod, b_mod, norm_g, ffn_w13, ffn_w2, attn_w_in, attn_q_g, attn_k_g, attn_w_out, hgrn_w_in, hgrn_lb_logits, hgrn_out_g, hgrn_w_out, ssd_w_in, ssd_conv_w, ssd_conv_b, ssd_dt_bias, ssd_a_log, ssd_d, ssd_norm_g, ssd_w_out, diff_w_in, diff_lambda, diff_subln_g, diff_w_out):
    b, n, d = x.shape
    n_ctx = ctx.shape[1]
    depth = w_mod.shape[0]
    d_ff = ffn_w2.shape[1]
    n_mixers = 4

    n_rows = -(-(b + 1) // 16) * 16
    c_all = jnp.zeros((n_rows, d), F32).at[:b].set(c).at[b].set(c_ctx)
    mods = adaln_all(c_all, w_mod, b_mod).reshape(depth, n_rows, N_MOD, d)

    x_lat = x
    x_ctx = ctx.reshape(1, b * n_ctx, d)
    for layer in range(depth):
        kind, j = layer % n_mixers, layer // n_mixers
        need_ctx = layer < depth - 1
        m3 = mods[layer]
        g = norm_g[layer]
        g_rows = [g[i].reshape(1, d) for i in range(4)]

        if kind == 0:
            perm = _deinterleave_perm(attn_w_in.shape[2], HEAD_128)
            n_qk = (attn_w_in.shape[2] // HEAD_128 // (GQA_GROUP + 2)) * (GQA_GROUP + 1) * HEAD_128
            perm = jnp.where(jnp.arange(perm.shape[0]) < n_qk, perm, jnp.arange(perm.shape[0]))
            w_in = attn_w_in[j][:, perm]
        elif kind == 1:
            w_in = hgrn_w_in[j]
        elif kind == 2:
            w_in = ssd_w_in[j]
        else:
            perm = _deinterleave_perm(diff_w_in.shape[2], LANES // 2)
            n_qk = 2 * (diff_w_in.shape[2] // 3)
            perm = jnp.where(jnp.arange(perm.shape[0]) < n_qk, perm, jnp.arange(perm.shape[0]))
            w_in = diff_w_in[j][:, perm]
        w_in = w_in.astype(BF16)
        n_proj = w_in.shape[1]
        tn = 1152 if n_proj % 1024 else 1024

        proj_l = inproj(x_lat, g_rows[0], m3, 0, 0, w_in, n_proj, tn, F32)
        proj_c = inproj(x_ctx, g_rows[0], m3, b, 0, w_in, n_proj, tn, F32).reshape(b, n_ctx, n_proj)
        out_args_l = (x_lat, g_rows[1], m3, 0, 2)
        out_args_c = (x_ctx, g_rows[1], m3, b, 2)

        if kind == 0:
            x_lat, x_ctx_new = _mixer_gqa(proj_l, proj_c, attn_w_out[j].astype(BF16), attn_q_g[j], attn_k_g[j],
                                          need_ctx, out_args_l, out_args_c)
        elif kind == 1:
            cum = jnp.cumsum(jax.nn.softmax(hgrn_lb_logits.astype(F32), axis=1), axis=1)
            lb = cum[:, layer] - cum[:, 0]
            x_lat, x_ctx_new = _mixer_hgrn(proj_l, proj_c, hgrn_w_out[j].astype(BF16), lb, hgrn_out_g[j],
                                           need_ctx, out_args_l, out_args_c)
        elif kind == 2:
            x_lat, x_ctx_new = _mixer_ssd(proj_l, proj_c, ssd_w_out[j].astype(BF16), ssd_conv_w[j], ssd_conv_b[j],
                                          ssd_dt_bias[j], ssd_a_log[j], ssd_d[j], ssd_norm_g[j], need_ctx,
                                          out_args_l, out_args_c)
        else:
            lambda_init = 0.8 - 0.6 * math.exp(-0.3 * layer)
            x_lat = _mixer_diff(proj_l, proj_c, diff_w_out[j].astype(BF16), diff_lambda[j], diff_subln_g[j],
                                lambda_init, out_args_l)
            x_ctx_new = None

        w13 = ffn_w13[layer].astype(BF16)
        w2 = ffn_w2[layer].astype(BF16)
        u_l = inproj(x_lat, g_rows[2], m3, 0, 3, w13, d_ff, 512, BF16, swiglu=True)
        x_lat = outproj("plain", [u_l], [0], [], w2, x_lat, g_rows[3], m3, 0, 5)
        if need_ctx:
            x_ctx = x_ctx_new
            u_c = inproj(x_ctx, g_rows[2], m3, b, 3, w13, d_ff, 512, BF16, swiglu=True)
            x_ctx = outproj("plain", [u_c], [0], [], w2, x_ctx, g_rows[3], m3, b, 5)
    return x_lat
```

```python
import functools
import math

import jax
import jax.numpy as jnp
from jax import lax
from jax.experimental import pallas as pl
from jax.experimental.pallas import tpu as pltpu

F32 = jnp.float32
BF16 = jnp.bfloat16

NORM_EPS = 1e-6
LOG2_E = math.log2(math.e)
ROPE_THETA = 10000.0
GRID_W = 64
N_MOD = 6
HEAD_128 = 128
GQA_GROUP = 4
HGRN_CHUNK = 32
HGRN_N_PROJ = 5
SSD_HEAD_DIM = 64
SSD_GROUPS = 8
SSD_STATE = 128
DIFF_HEADS = 16

LANES = 128
VMEM_LIMIT_BYTES = 56 * 2**20
TM_IN = 1024
TM_OUT = 512
ROW_CHUNK = 16
RESIDENT_K_FULL_TILE = 4096
READOUT_SPLIT = 2
ROW_UNROLL = 16
TQ_GQA = 512
TQ_DIFF = 1024
HGRN_BLOCK = 256
HGRN_HEADS_PER_STEP = 8
SSD_CHUNK = 128


def _cparams(n_axes):
    return pltpu.CompilerParams(dimension_semantics=("arbitrary",) * n_axes,
                                vmem_limit_bytes=VMEM_LIMIT_BYTES)


def _silu(x):
    return x / (1.0 + jnp.exp(-x))


def _dot(a, b):
    return jnp.dot(a, b, preferred_element_type=F32)


def _dot_nt(a, b):
    return lax.dot_general(a, b, (((1,), (1,)), ((), ())), preferred_element_type=F32)


def _dot_tn(a, b):
    return lax.dot_general(a, b, (((0,), (0,)), ((), ())), preferred_element_type=F32)


def _dot_exact_lhs(m01, x, terms=3):
    hi = x.astype(BF16)
    r1 = x - hi.astype(F32)
    mid = r1.astype(BF16)
    out = _dot(m01, hi) + _dot(m01, mid)
    if terms == 3:
        out = out + _dot(m01, (r1 - mid.astype(F32)).astype(BF16))
    return out


def _adaln_kernel(c_ref, w_ref, b_ref, o_ref):
    s = _silu(c_ref[...]).astype(BF16)
    o_ref[0] = _dot(s, w_ref[0].astype(BF16)) + b_ref[0]


def adaln_all(c_all, w_mod, b_mod):
    n_layers, d, n = w_mod.shape
    r = c_all.shape[0]
    tn = 1024
    return pl.pallas_call(
        _adaln_kernel,
        grid=(n_layers, n // tn),
        in_specs=[pl.BlockSpec((r, d), lambda l, j: (0, 0)),
                  pl.BlockSpec((1, d, tn), lambda l, j: (l, 0, j)),
                  pl.BlockSpec((1, 1, tn), lambda l, j: (l, 0, j))],
        out_specs=pl.BlockSpec((1, r, tn), lambda l, j: (l, 0, j)),
        out_shape=jax.ShapeDtypeStruct((n_layers, r, n), F32),
        compiler_params=_cparams(2),
        name="adaln",
    )(c_all, w_mod, b_mod.reshape(n_layers, 1, n))


def _inproj_kernel(x_ref, g_ref, m_ref, *rest, shift_row, n_w, swiglu):
    w_refs = rest[:n_w]
    o_ref = rest[n_w]
    h_ref = rest[n_w + 1]

    @pl.when(pl.program_id(2) == 0)
    def _():
        shift = m_ref[0, shift_row:shift_row + 1, :]
        gain = g_ref[...] * (1.0 + m_ref[0, shift_row + 1:shift_row + 2, :])

        def rows(i, carry):
            r0 = pl.multiple_of(i * ROW_CHUNK, ROW_CHUNK)
            x = x_ref[0, pl.ds(r0, ROW_CHUNK), :]
            ms = jnp.mean(x * x, axis=-1, keepdims=True)
            h_ref[pl.ds(r0, ROW_CHUNK), :] = (x * lax.rsqrt(ms + NORM_EPS) * gain + shift).astype(BF16)
            return carry

        lax.fori_loop(0, h_ref.shape[0] // ROW_CHUNK, rows, 0, unroll=ROW_UNROLL)

    h = h_ref[...]
    if swiglu:
        half = o_ref.shape[2] // 2
        for c in range(2):
            cs = slice(c * half, (c + 1) * half)
            gate = _dot(h, w_refs[0][:, cs])
            up = _dot(h, w_refs[1][:, cs])
            o_ref[0, :, cs] = (_silu(gate) * up).astype(o_ref.dtype)
    else:
        o_ref[0] = _dot(h, w_refs[0][...]).astype(o_ref.dtype)


def inproj(x3, g_row, mods3, mod_row0, shift_row, w, n_out, tn, out_dtype, swiglu=False):
    bx, s, d = x3.shape
    tm = min(TM_IN, s)
    offs = (0, n_out // tn) if swiglu else (0,)
    in_specs = [pl.BlockSpec((1, tm, d), lambda b, i, j: (b, i, 0)),
                pl.BlockSpec((1, d), lambda b, i, j: (0, 0)),
                pl.BlockSpec((1, N_MOD, d), lambda b, i, j: (b + mod_row0, 0, 0))]
    for off in offs:
        in_specs.append(pl.BlockSpec((d, tn), lambda b, i, j, off=off: (0, j + off)))
    return pl.pallas_call(
        functools.partial(_inproj_kernel, shift_row=shift_row, n_w=len(offs), swiglu=swiglu),
        grid=(bx, s // tm, n_out // tn),
        in_specs=in_specs,
        out_specs=pl.BlockSpec((1, tm, tn), lambda b, i, j: (b, i, j)),
        out_shape=jax.ShapeDtypeStruct((bx, s, n_out), out_dtype),
        scratch_shapes=[pltpu.VMEM((tm, d), BF16)],
        compiler_params=_cparams(3),
        name="inproj_swiglu" if swiglu else "inproj",
    )(x3, g_row, mods3, *([w] * len(offs)))


def _outproj_kernel(*refs, mode, gate_row, norm_group):
    n_in = {"plain": 1, "hgrn": 4, "ssd": 3}[mode]
    ins = refs[:n_in]
    w_ref, res_ref, g_ref, m_ref, o_ref, acc_ref = refs[n_in:]

    def readout(rs):
        if mode == "plain":
            return ins[0][0, rs, :]
        if mode == "hgrn":
            of_ref, ob_ref, gp_ref, og_ref = ins
            o = of_ref[0, rs, :] + ob_ref[0, rs, :]
            pieces = []
            for h in range(o.shape[-1] // HEAD_128):
                sl = slice(h * HEAD_128, (h + 1) * HEAD_128)
                oh = o[:, sl]
                ms = jnp.mean(oh * oh, axis=-1, keepdims=True)
                pieces.append(oh * lax.rsqrt(ms + NORM_EPS) * og_ref[:, sl])
            return (jnp.concatenate(pieces, axis=-1) * _silu(gp_ref[0, rs, :])).astype(BF16)
        y_ref, z_ref, ng_ref = ins
        pieces = []
        for gi in range(y_ref.shape[2] // norm_group):
            sl = slice(gi * norm_group, (gi + 1) * norm_group)
            y = y_ref[0, rs, sl] * _silu(z_ref[0, rs, sl])
            ms = jnp.mean(y * y, axis=-1, keepdims=True)
            pieces.append((y * lax.rsqrt(ms + NORM_EPS) * ng_ref[:, sl]).astype(BF16))
        return pieces[0] if len(pieces) == 1 else jnp.concatenate(pieces, axis=-1)

    tm = acc_ref.shape[0]
    n_split = 1 if mode == "plain" else READOUT_SPLIT
    for s in range(n_split):
        rs = slice(s * tm // n_split, (s + 1) * tm // n_split)
        acc_ref[rs, :] = _dot(readout(rs), w_ref[...])

    gain = g_ref[...] * m_ref[0, gate_row:gate_row + 1, :]

    def rows(i, carry):
        r0 = pl.multiple_of(i * ROW_CHUNK, ROW_CHUNK)
        y = acc_ref[pl.ds(r0, ROW_CHUNK), :]
        ms = jnp.mean(y * y, axis=-1, keepdims=True)
        o_ref[0, pl.ds(r0, ROW_CHUNK), :] = (res_ref[0, pl.ds(r0, ROW_CHUNK), :]
                                             + y * lax.rsqrt(ms + NORM_EPS) * gain)
        return carry

    lax.fori_loop(0, tm // ROW_CHUNK, rows, 0, unroll=ROW_UNROLL)


def outproj(mode, ins, in_cols, row_ins, w, res3, g_row, mods3, mod_row0, gate_row, norm_group=0):
    bx, s, d = res3.shape
    kdim = w.shape[0]
    tm = min(TM_OUT if (mode == "plain" and kdim <= RESIDENT_K_FULL_TILE) else TM_OUT // 2, s)
    in_specs = [pl.BlockSpec((1, tm, kdim), lambda b, i, off=c // kdim: (b, i, off)) for c in in_cols]
    in_specs += [pl.BlockSpec((1, kdim), lambda b, i: (0, 0)) for _ in row_ins]
    in_specs += [pl.BlockSpec((kdim, d), lambda b, i: (0, 0), pipeline_mode=pl.Buffered(1)),
                 pl.BlockSpec((1, tm, d), lambda b, i: (b, i, 0)),
                 pl.BlockSpec((1, d), lambda b, i: (0, 0)),
                 pl.BlockSpec((1, N_MOD, d), lambda b, i: (b + mod_row0, 0, 0))]
    return pl.pallas_call(
        functools.partial(_outproj_kernel, mode=mode, gate_row=gate_row, norm_group=norm_group),
        grid=(bx, s // tm),
        in_specs=in_specs,
        out_specs=pl.BlockSpec((1, tm, d), lambda b, i: (b, i, 0)),
        out_shape=jax.ShapeDtypeStruct((bx, s, d), F32),
        scratch_shapes=[pltpu.VMEM((tm, d), F32)],
        compiler_params=_cparams(2),
        name="outproj_" + mode,
    )(*ins, *row_ins, w, res3, g_row, mods3)


def _deinterleave_perm(width, block):
    idx = jnp.arange(width).reshape(width // block, block)
    return jnp.concatenate([idx[:, 0::2], idx[:, 1::2]], axis=1).reshape(width)


def _rope_tables(n, head_dim, reps):
    rows = n // GRID_W
    row = jnp.repeat(jnp.arange(rows, dtype=F32), GRID_W)
    col = jnp.tile(jnp.arange(GRID_W, dtype=F32), rows)
    axis_dim = head_dim // 2
    inv_freq = ROPE_THETA ** (-jnp.arange(0, axis_dim, 2, dtype=F32) / axis_dim)
    ang = jnp.concatenate([row[:, None] * inv_freq, col[:, None] * inv_freq], axis=-1)
    cos, sin = jnp.cos(ang), jnp.sin(ang)
    cos_t = jnp.tile(jnp.concatenate([cos, cos], axis=-1), (1, reps))
    sin_t = jnp.tile(jnp.concatenate([-sin, sin], axis=-1), (1, reps))
    return cos_t, sin_t


def _identity_tables(n):
    return jnp.ones((n, LANES), F32), jnp.zeros((n, LANES), F32)


def _rot_half_128(y):
    return pltpu.roll(y, 64, 1)


def _rot_half_64(y):
    lane = lax.broadcasted_iota(jnp.int32, y.shape, 1)
    first = jnp.bitwise_and(lane, 63) < 32
    return jnp.where(first, pltpu.roll(y, 96, 1), pltpu.roll(y, 32, 1))


def _col_reduce(x, op):
    n, t = x.shape
    if n > LANES:
        x = op(x.reshape(n // LANES, LANES, t), axis=0)
    return op(x, axis=0, keepdims=True)


def _softmax_numerators_t(scores):
    m = functools.reduce(jnp.maximum, [_col_reduce(sc, jnp.max) for sc in scores])
    ps = [jnp.exp2(sc - m) for sc in scores]
    l = functools.reduce(jnp.add, [_col_reduce(p, jnp.sum) for p in ps])
    return ps, l


def _softmax_pv_t(scores, v_t):
    ps, l = _softmax_numerators_t(scores)
    o_t = functools.reduce(jnp.add, [_dot(v, p.astype(BF16)) for v, p in zip(v_t, ps)])
    return o_t, l


def _gqa_norm_rope(x, gain, cos, sin):
    ms = jnp.mean(x * x, axis=-1, keepdims=True)
    y = x * lax.rsqrt(ms + NORM_EPS) * gain
    return y * cos + _rot_half_128(y) * sin


def _gqa_kernel(q_ref, cq_ref, sq_ref, qg_ref, kg_ref, *rest, n_seg, group, scale):
    seg_refs = rest[:4 * n_seg]
    o_ref = rest[4 * n_seg]
    scr = rest[4 * n_seg + 1:]

    @pl.when(pl.program_id(2) == 0)
    def _():
        for s in range(n_seg):
            k_ref, v_ref, ck_ref, sk_ref = seg_refs[4 * s:4 * s + 4]
            scr[2 * s][...] = _gqa_norm_rope(k_ref[0], kg_ref[...], ck_ref[...], sk_ref[...]).astype(BF16)
            scr[2 * s + 1][...] = v_ref[0].T.astype(BF16)

    heads = [slice(g * HEAD_128, (g + 1) * HEAD_128) for g in range(group)]
    qn = [(_gqa_norm_rope(q_ref[0, :, sl], qg_ref[...], cq_ref[...], sq_ref[...]) * (scale * LOG2_E)).astype(BF16)
          for sl in heads]
    def qk(g):
        return [_dot_nt(scr[2 * s][...], qn[g]) for s in range(n_seg)]

    scores = qk(0)
    for g, sl in enumerate(heads):
        nxt = qk(g + 1) if g + 1 < group else None
        o_t, l = _softmax_pv_t(scores, [scr[2 * s + 1][...] for s in range(n_seg)])
        o_ref[0, :, sl] = (o_t * (1.0 / l)).T.astype(o_ref.dtype)
        scores = nxt


def gqa_attention(q_src, q_tabs, segs, q_gain, k_gain, n_kv):
    b, sq, _ = q_src.shape
    group = GQA_GROUP
    tq = min(TQ_GQA, sq)
    qw = group * HEAD_128
    k_blk0 = n_kv * group
    v_blk0 = k_blk0 + n_kv
    in_specs = [pl.BlockSpec((1, tq, qw), lambda bi, h, i: (bi, i, h)),
                pl.BlockSpec((tq, LANES), lambda bi, h, i: (i, 0)),
                pl.BlockSpec((tq, LANES), lambda bi, h, i: (i, 0)),
                pl.BlockSpec((1, LANES), lambda bi, h, i: (0, 0)),
                pl.BlockSpec((1, LANES), lambda bi, h, i: (0, 0))]
    args = [q_src, q_tabs[0], q_tabs[1], q_gain, k_gain]
    scratch = []
    for kv_src, (ck, sk) in segs:
        ks = kv_src.shape[1]
        in_specs += [pl.BlockSpec((1, ks, LANES), lambda bi, h, i: (bi, 0, k_blk0 + h)),
                     pl.BlockSpec((1, ks, LANES), lambda bi, h, i: (bi, 0, v_blk0 + h)),
                     pl.BlockSpec((ks, LANES), lambda bi, h, i: (0, 0)),
                     pl.BlockSpec((ks, LANES), lambda bi, h, i: (0, 0))]
        args += [kv_src, kv_src, ck, sk]
        scratch += [pltpu.VMEM((ks, LANES), BF16), pltpu.VMEM((LANES, ks), BF16)]
    return pl.pallas_call(
        functools.partial(_gqa_kernel, n_seg=len(segs), group=group, scale=HEAD_128 ** -0.5),
        grid=(b, n_kv, sq // tq),
        in_specs=in_specs,
        out_specs=pl.BlockSpec((1, tq, qw), lambda bi, h, i: (bi, i, h)),
        out_shape=jax.ShapeDtypeStruct((b, sq, n_kv * qw), BF16),
        scratch_shapes=scratch,
        compiler_params=_cparams(3),
        name="gqa_attention",
    )(*args)


def _diff_kernel(q_ref, cq_ref, sq_ref, lam_ref, sg_ref, *rest, n_seg, scale, lambda_init):
    seg_refs = rest[:4 * n_seg]
    o_ref = rest[4 * n_seg]
    scr = rest[4 * n_seg + 1:]

    @pl.when(pl.program_id(2) == 0)
    def _():
        for s in range(n_seg):
            k_ref, v_ref, ck_ref, sk_ref = seg_refs[4 * s:4 * s + 4]
            k = k_ref[0]
            scr[2 * s][...] = (k * ck_ref[...] + _rot_half_64(k) * sk_ref[...]).astype(BF16)
            scr[2 * s + 1][...] = v_ref[0].T.astype(BF16)

    q = q_ref[0]
    q = (q * cq_ref[...] + _rot_half_64(q) * sq_ref[...]) * (scale * LOG2_E)
    lane = lax.broadcasted_iota(jnp.int32, q.shape, 1)
    zero = jnp.zeros_like(q)
    qs = [jnp.where(lane < 64, q, zero).astype(BF16), jnp.where(lane < 64, zero, q).astype(BF16)]

    lp = lam_ref[...]
    lam = (jnp.exp(jnp.sum(lp[0:1] * lp[1:2], axis=-1, keepdims=True))
           - jnp.exp(jnp.sum(lp[2:3] * lp[3:4], axis=-1, keepdims=True)) + lambda_init)

    scores = [[_dot_nt(scr[2 * s][...], qc) for s in range(n_seg)] for qc in qs]
    (p1, l1), (p2, l2) = [_softmax_numerators_t(sc) for sc in scores]
    r = lam * l1 / l2
    o_t = functools.reduce(jnp.add, [
        _dot(scr[2 * s + 1][...], (p1[s] - r * p2[s]).astype(BF16)) for s in range(n_seg)])
    o = (o_t * (1.0 / l1)).T
    ms = jnp.mean(o * o, axis=-1, keepdims=True)
    o_ref[0] = (o * lax.rsqrt(ms + NORM_EPS) * sg_ref[...] * (1.0 - lambda_init)).astype(o_ref.dtype)


def diff_attention(q_src, q_tabs, segs, lam_p, subln_g, lambda_init):
    b, sq, _ = q_src.shape
    nh = DIFF_HEADS
    tq = min(TQ_DIFF, sq)
    in_specs = [pl.BlockSpec((1, tq, LANES), lambda bi, h, i: (bi, i, h)),
                pl.BlockSpec((tq, LANES), lambda bi, h, i: (i, 0)),
                pl.BlockSpec((tq, LANES), lambda bi, h, i: (i, 0)),
                pl.BlockSpec(lam_p.shape, lambda bi, h, i: (0, 0)),
                pl.BlockSpec((1, LANES), lambda bi, h, i: (0, 0))]
    args = [q_src, q_tabs[0], q_tabs[1], lam_p, subln_g]
    scratch = []
    for kv_src, (ck, sk) in segs:
        ks = kv_src.shape[1]
        in_specs += [pl.BlockSpec((1, ks, LANES), lambda bi, h, i: (bi, 0, nh + h)),
                     pl.BlockSpec((1, ks, LANES), lambda bi, h, i: (bi, 0, 2 * nh + h)),
                     pl.BlockSpec((ks, LANES), lambda bi, h, i: (0, 0)),
                     pl.BlockSpec((ks, LANES), lambda bi, h, i: (0, 0))]
        args += [kv_src, kv_src, ck, sk]
        scratch += [pltpu.VMEM((ks, LANES), BF16), pltpu.VMEM((LANES, ks), BF16)]
    return pl.pallas_call(
        functools.partial(_diff_kernel, n_seg=len(segs), scale=(LANES // 2) ** -0.5, lambda_init=lambda_init),
        grid=(b, nh, sq // tq),
        in_specs=in_specs,
        out_specs=pl.BlockSpec((1, tq, LANES), lambda bi, h, i: (bi, i, h)),
        out_shape=jax.ShapeDtypeStruct((b, sq, nh * LANES), BF16),
        scratch_shapes=scratch,
        compiler_params=_cparams(3),
        name="diff_attention",
    )(*args)


def _hgrn_kernel(q_ref, f_ref, v_ref, lb_ref, s0_ref, o_ref, sout_ref, st_ref, *, reverse, hpb, blk_len, nblk):
    blk = pl.program_id(2)

    @pl.when(blk == 0)
    def _():
        st_ref[...] = s0_ref[0]

    n_chunks = blk_len // HGRN_CHUNK
    row = lax.broadcasted_iota(jnp.int32, (blk_len, blk_len), 0)
    col = lax.broadcasted_iota(jnp.int32, (blk_len, blk_len), 1)
    same = lax.shift_right_logical(row, 5) == lax.shift_right_logical(col, 5)
    tri = jnp.logical_and(same, (col >= row) if reverse else (col <= row))
    tri01 = tri.astype(F32).astype(BF16)
    heads = [slice(h * HEAD_128, (h + 1) * HEAD_128) for h in range(hpb)]
    chunks = [slice(c * HGRN_CHUNK, (c + 1) * HGRN_CHUNK) for c in range(n_chunks)]

    q = _silu(q_ref[0])
    lb = lb_ref[0]
    forget = lb + (1.0 - lb) * (1.0 / (1.0 + jnp.exp(-f_ref[0])))
    k = 1.0 - forget
    cum = _dot_exact_lhs(tri01, jnp.log(forget), terms=2)
    last_rows = [cum[c.start:c.start + 1] if reverse else cum[c.stop - 1:c.stop] for c in chunks]
    last = jnp.concatenate([jnp.broadcast_to(r, (HGRN_CHUNK, r.shape[1])) for r in last_rows], axis=0)
    q_dec = (q * jnp.exp(cum)).astype(BF16)
    k_inv = (k * jnp.exp(-cum)).astype(BF16)
    k_end = (k * jnp.exp(last - cum)).astype(BF16)
    vb = v_ref[0].astype(BF16)
    e_last = [jnp.exp(r) for r in last_rows]
    att = [jnp.where(tri, _dot_nt(q_dec[:, sl], k_inv[:, sl]), 0.0).astype(BF16) for sl in heads]
    o_intra = [_dot(att[h], vb[:, sl]) for h, sl in enumerate(heads)]
    upd = [[_dot_tn(vb[rs, sl], k_end[rs, sl]) for sl in heads] for rs in chunks]
    st = [st_ref[h] for h in range(hpb)]
    for ci in range(n_chunks):
        c = n_chunks - 1 - ci if reverse else ci
        rs = chunks[c]
        for h, sl in enumerate(heads):
            o_ref[0, rs, sl] = o_intra[h][rs] + _dot_nt(q_dec[rs, sl], st[h].astype(BF16))
            st[h] = st[h] * e_last[c][:, sl] + upd[c][h]
    for h in range(hpb):
        st_ref[h] = st[h]

    @pl.when(blk == nblk - 1)
    def _():
        sout_ref[0] = st_ref[...]


def hgrn_scan(proj, lb3, s0, direction):
    b, s, w = proj.shape
    d = w // HGRN_N_PROJ
    nh = d // HEAD_128
    hpb = HGRN_HEADS_PER_STEP
    hw = hpb * HEAD_128
    blk_len = min(HGRN_BLOCK, s)
    nblk = s // blk_len
    ncb = d // hw
    reverse = direction == 1

    def tok(j):
        return nblk - 1 - j if reverse else j

    in_specs = [pl.BlockSpec((1, blk_len, hw), lambda bi, h, j: (bi, tok(j), h)),
                pl.BlockSpec((1, blk_len, hw), lambda bi, h, j: (bi, tok(j), (1 + direction) * ncb + h)),
                pl.BlockSpec((1, blk_len, hw), lambda bi, h, j: (bi, tok(j), 3 * ncb + h)),
                pl.BlockSpec((1, 1, hw), lambda bi, h, j: (direction, 0, h)),
                pl.BlockSpec((1, hpb, HEAD_128, HEAD_128), lambda bi, h, j: (bi, h, 0, 0))]
    return pl.pallas_call(
        functools.partial(_hgrn_kernel, reverse=reverse, hpb=hpb, blk_len=blk_len, nblk=nblk),
        grid=(b, nh // hpb, nblk),
        in_specs=in_specs,
        out_specs=[pl.BlockSpec((1, blk_len, hw), lambda bi, h, j: (bi, tok(j), h)),
                   pl.BlockSpec((1, hpb, HEAD_128, HEAD_128), lambda bi, h, j: (bi, h, 0, 0))],
        out_shape=[jax.ShapeDtypeStruct((b, s, d), F32),
                   jax.ShapeDtypeStruct((b, nh, HEAD_128, HEAD_128), F32)],
        scratch_shapes=[pltpu.VMEM((hpb, HEAD_128, HEAD_128), F32)],
        compiler_params=_cparams(3),
        name="hgrn_scan",
    )(proj, proj, proj, lb3, s0)


def _conv_silu_kernel(x_ref, w_ref, b_ref, o_ref):
    u = x_ref[0]
    n = u.shape[0]
    row = lax.broadcasted_iota(jnp.int32, u.shape, 0)
    prev = jnp.where(row == 0, 0.0, pltpu.roll(u, 1, 0))
    nxt = jnp.where(row == n - 1, 0.0, pltpu.roll(u, n - 1, 0))
    y = prev * w_ref[0:1, :] + u * w_ref[1:2, :] + nxt * w_ref[2:3, :] + b_ref[...]
    o_ref[0] = _silu(y).astype(o_ref.dtype)


def conv_silu(proj, col_blk0, width, conv_w, conv_b, tc=512):
    b, s, _ = proj.shape
    return pl.pallas_call(
        _conv_silu_kernel,
        grid=(b, width // tc),
        in_specs=[pl.BlockSpec((1, s, tc), lambda bi, j: (bi, 0, col_blk0 + j)),
                  pl.BlockSpec((3, tc), lambda bi, j: (0, j)),
                  pl.BlockSpec((1, tc), lambda bi, j: (0, j))],
        out_specs=pl.BlockSpec((1, s, tc), lambda bi, j: (bi, 0, j)),
        out_shape=jax.ShapeDtypeStruct((b, s, width), BF16),
        compiler_params=_cparams(2),
        name="ssd_conv_silu",
    )(proj, conv_w, conv_b)


def _ssd_kernel(x_ref, bc_ref, dt_ref, dtb_ref, alog_ref, dsk_ref, h0_ref, *rest,
                reverse, direction, nck, n_groups, hpg, has_prev):
    yprev_ref = rest[0] if has_prev else None
    y_ref, hout_ref, st_ref = rest[-3:]
    ck = pl.program_id(1)
    lc = SSD_CHUNK
    gw = hpg * SSD_HEAD_DIM
    n_heads = n_groups * hpg

    @pl.when(ck == 0)
    def _():
        st_ref[...] = h0_ref[0]

    row = lax.broadcasted_iota(jnp.int32, (lc, lc), 0)
    col = lax.broadcasted_iota(jnp.int32, (lc, lc), 1)
    tri = (col >= row) if reverse else (col <= row)
    tri01 = tri.astype(F32).astype(BF16)
    lo = lax.broadcasted_iota(jnp.int32, (lc, LANES), 1) < SSD_HEAD_DIM
    lo_row = lo[0:1, :]

    x_in = dt_ref[0] + dtb_ref[...]
    dt = jnp.maximum(x_in, 0.0) + jnp.log1p(jnp.exp(-jnp.abs(x_in)))
    a = -jnp.exp(alog_ref[...])
    cum = _dot_exact_lhs(tri01, dt * a)
    cum_t = cum.T
    dt_t = dt.T
    t_last = 0 if reverse else lc - 1

    for g in range(n_groups):
        b_g = bc_ref[0, :, g * SSD_STATE:(g + 1) * SSD_STATE]
        c_g = bc_ref[0, :, (n_groups + g) * SSD_STATE:(n_groups + g + 1) * SSD_STATE]
        b_t = b_g.astype(F32).T
        cb = _dot_nt(c_g, b_g)
        h_g = st_ref[g]
        y_state = _dot(c_g, h_g.astype(BF16))
        for p in range(hpg // 2):
            cols = slice(g * gw + p * LANES, g * gw + (p + 1) * LANES)
            x2 = x_ref[0, :, cols].astype(F32)
            zero = jnp.zeros_like(x2)
            x_halves = [jnp.where(lo, x2, zero).astype(BF16), jnp.where(lo, zero, x2).astype(BF16)]
            y_parts, upd_parts, e_cols, decays = [], [], [], []
            for jj in range(2):
                r = direction * n_heads + g * hpg + 2 * p + jj
                row_b = jnp.broadcast_to(cum_t[r:r + 1, :], (lc, lc))
                col_b = row_b.T
                dt_row = dt_t[r:r + 1, :]
                decay = jnp.exp(jnp.where(tri, col_b - row_b, -jnp.inf))
                w = (cb * decay * dt_row).astype(BF16)
                y_parts.append(_dot(w, x_halves[jj]))
                last_b = col_b[t_last:t_last + 1, :]
                coef_row = jnp.exp(last_b - row_b[0:1, :]) * dt_row
                upd_parts.append(_dot((b_t * coef_row).astype(BF16), x_halves[jj]))
                e_cols.append(jnp.exp(col_b))
                decays.append(jnp.exp(last_b))
            cs = slice(p * LANES, (p + 1) * LANES)
            base = yprev_ref[0, :, cols] if has_prev else dsk_ref[:, cols] * x2
            y_ref[0, :, cols] = base + (
                y_parts[0] + y_parts[1] + y_state[:, cs] * jnp.where(lo, e_cols[0], e_cols[1]))
            st_ref[g, :, cs] = (h_g[:, cs] * jnp.where(lo_row, decays[0], decays[1])
                                + upd_parts[0] + upd_parts[1])

    @pl.when(ck == nck - 1)
    def _():
        hout_ref[0] = st_ref[...]


def ssd_scan(xs, bc, proj, dt_blk, dt_bias, a_log, d_skip, h0, y_prev, direction):
    b, s, d_inner = xs.shape
    has_prev = y_prev is not None
    n_groups = SSD_GROUPS
    hpg = d_inner // SSD_HEAD_DIM // n_groups
    gw = hpg * SSD_HEAD_DIM
    lc = SSD_CHUNK
    nck = s // lc
    reverse = direction == 1

    def tok(j):
        return nck - 1 - j if reverse else j

    seq_spec = pl.BlockSpec((1, lc, d_inner), lambda bi, j: (bi, tok(j), 0))
    return pl.pallas_call(
        functools.partial(_ssd_kernel, reverse=reverse, direction=direction, nck=nck, n_groups=n_groups, hpg=hpg,
                          has_prev=has_prev),
        grid=(b, nck),
        in_specs=[seq_spec,
                  pl.BlockSpec((1, lc, bc.shape[2]), lambda bi, j: (bi, tok(j), 0)),
                  pl.BlockSpec((1, lc, LANES), lambda bi, j: (bi, tok(j), dt_blk)),
                  pl.BlockSpec((1, LANES), lambda bi, j: (0, 0)),
                  pl.BlockSpec((1, LANES), lambda bi, j: (0, 0)),
                  pl.BlockSpec((1, d_inner), lambda bi, j: (0, 0)),
                  pl.BlockSpec((1, n_groups, SSD_STATE, gw), lambda bi, j: (bi, 0, 0, 0))]
        + ([seq_spec] if has_prev else []),
        out_specs=[pl.BlockSpec((1, lc, d_inner), lambda bi, j: (bi, tok(j), 0)),
                   pl.BlockSpec((1, n_groups, SSD_STATE, gw), lambda bi, j: (bi, 0, 0, 0))],
        out_shape=[jax.ShapeDtypeStruct((b, s, d_inner), F32),
                   jax.ShapeDtypeStruct((b, n_groups, SSD_STATE, gw), F32)],
        scratch_shapes=[pltpu.VMEM((n_groups, SSD_STATE, gw), F32)],
        compiler_params=_cparams(2),
        name="ssd_scan",
    )(xs, bc, proj, dt_bias, a_log, d_skip, h0, *([y_prev] if has_prev else []))


def _mixer_gqa(proj_l, proj_c, w_out, q_g, k_g, need_ctx, out_args_l, out_args_c):
    n = proj_l.shape[1]
    c = proj_c.shape[1]
    n_kv = proj_l.shape[2] // HEAD_128 // (GQA_GROUP + 2)
    perm = _deinterleave_perm(HEAD_128, HEAD_128)
    q_gain = q_g[perm].reshape(1, HEAD_128)
    k_gain = k_g[perm].reshape(1, HEAD_128)
    lat_tabs = _rope_tables(n, HEAD_128, 1)
    ctx_tabs = _identity_tables(c)
    o_l = gqa_attention(proj_l, lat_tabs, [(proj_l, lat_tabs), (proj_c, ctx_tabs)], q_gain, k_gain, n_kv)
    x_l = outproj("plain", [o_l], [0], [], w_out, *out_args_l)
    x_c = None
    if need_ctx:
        o_c = gqa_attention(proj_c, ctx_tabs, [(proj_c, ctx_tabs)], q_gain, k_gain, n_kv)
        x_c = outproj("plain", [o_c.reshape(1, -1, o_c.shape[2])], [0], [], w_out, *out_args_c)
    return x_l, x_c


def _mixer_hgrn(proj_l, proj_c, w_out, lb, out_g, need_ctx, out_args_l, out_args_c):
    b, _, w = proj_l.shape
    d = w // HGRN_N_PROJ
    nh = d // HEAD_128
    lb3 = lb.reshape(2, 1, d)
    s0 = jnp.zeros((b, nh, HEAD_128, HEAD_128), F32)
    o_l, o_c = [], []
    for direction in range(2):
        oc, s_ctx = hgrn_scan(proj_c, lb3, s0, direction)
        ol, _ = hgrn_scan(proj_l, lb3, s_ctx, direction)
        o_l.append(ol)
        o_c.append(oc)
    og = jnp.tile(out_g, nh).reshape(1, d)
    g_col0 = 4 * d
    x_l = outproj("hgrn", [o_l[0], o_l[1], proj_l], [0, 0, g_col0], [og], w_out, *out_args_l)
    x_c = None
    if need_ctx:
        flat = lambda t: t.reshape(1, -1, t.shape[2])
        x_c = outproj("hgrn", [flat(o_c[0]), flat(o_c[1]), flat(proj_c)], [0, 0, g_col0], [og], w_out,
                      *out_args_c)
    return x_l, x_c


def _mixer_ssd(proj_l, proj_c, w_out, conv_w, conv_b, dt_bias, a_log, d_skip, norm_g, need_ctx,
               out_args_l, out_args_c):
    b = proj_l.shape[0]
    d_inner = w_out.shape[0]
    n_heads = d_inner // SSD_HEAD_DIM
    hpg = n_heads // SSD_GROUPS
    gn = SSD_GROUPS * SSD_STATE
    tc = 512
    x_blk0 = d_inner // tc
    dt_blk = (2 * d_inner + 2 * gn) // LANES
    cw_x, cw_bc = conv_w[:, :d_inner], conv_w[:, d_inner:]
    cb_x, cb_bc = conv_b[:d_inner].reshape(1, -1), conv_b[d_inner:].reshape(1, -1)
    dtb = dt_bias.reshape(1, 2 * n_heads)
    alog = a_log.reshape(1, 2 * n_heads)
    h0 = jnp.zeros((b, SSD_GROUPS, SSD_STATE, hpg * SSD_HEAD_DIM), F32)

    def prep(proj):
        xs = conv_silu(proj, x_blk0, d_inner, cw_x, cb_x, tc)
        bc = conv_silu(proj, x_blk0 + d_inner // tc, 2 * gn, cw_bc, cb_bc, tc)
        return xs, bc

    xs_l, bc_l = prep(proj_l)
    xs_c, bc_c = prep(proj_c)
    dsk = jnp.repeat(d_skip, SSD_HEAD_DIM).reshape(1, d_inner)
    y_l, y_c = None, None
    for direction in range(2):
        y_c, h_ctx = ssd_scan(xs_c, bc_c, proj_c, dt_blk, dtb, alog, dsk, h0, y_c, direction)
        y_l, _ = ssd_scan(xs_l, bc_l, proj_l, dt_blk, dtb, alog, dsk, h_ctx, y_l, direction)
    ng = norm_g.reshape(1, d_inner)
    ngrp = d_inner // SSD_GROUPS
    x_l = outproj("ssd", [y_l, proj_l], [0, 0], [ng], w_out, *out_args_l, norm_group=ngrp)
    x_c = None
    if need_ctx:
        flat = lambda t: t.reshape(1, -1, t.shape[2])
        x_c = outproj("ssd", [flat(y_c), flat(proj_c)], [0, 0], [ng], w_out, *out_args_c, norm_group=ngrp)
    return x_l, x_c


def _mixer_diff(proj_l, proj_c, w_out, lam_p, subln_g, lambda_init, out_args_l):
    n = proj_l.shape[1]
    c = proj_c.shape[1]
    lat_tabs = _rope_tables(n, LANES // 2, 2)
    ctx_tabs = _identity_tables(c)
    o_l = diff_attention(proj_l, lat_tabs, [(proj_l, lat_tabs), (proj_c, ctx_tabs)], lam_p,
                         subln_g.reshape(1, LANES), lambda_init)
    return outproj("plain", [o_l], [0], [], w_out, *out_args_l)


def kernel(x, c, ctx, c_ctx, w_mod, b_mod, norm_g, ffn_w13, ffn_w2, attn_w_in, attn_q_g, attn_k_g, attn_w_out, hgrn_w_in, hgrn_lb_logits, hgrn_out_g, hgrn_w_out, ssd_w_in, ssd_conv_w, ssd_conv_b, ssd_dt_bias, ssd_a_log, ssd_d, ssd_norm_g, ssd_w_out, diff_w_in, diff_lambda, diff_subln_g, diff_w_out):
    b, n, d = x.shape
    n_ctx = ctx.shape[1]
    depth = w_mod.shape[0]
    d_ff = ffn_w2.shape[1]
    n_mixers = 4

    n_rows = -(-(b + 1) // 16) * 16
    c_all = jnp.zeros((n_rows, d), F32).at[:b].set(c).at[b].set(c_ctx)
    mods = adaln_all(c_all, w_mod, b_mod).reshape(depth, n_rows, N_MOD, d)

    x_lat = x
    x_ctx = ctx.reshape(1, b * n_ctx, d)
    for layer in range(depth):
        kind, j = layer % n_mixers, layer // n_mixers
        need_ctx = layer < depth - 1
        m3 = mods[layer]
        g = norm_g[layer]
        g_rows = [g[i].reshape(1, d) for i in range(4)]

        if kind == 0:
            perm = _deinterleave_perm(attn_w_in.shape[2], HEAD_128)
            n_qk = (attn_w_in.shape[2] // HEAD_128 // (GQA_GROUP + 2)) * (GQA_GROUP + 1) * HEAD_128
            perm = jnp.where(jnp.arange(perm.shape[0]) < n_qk, perm, jnp.arange(perm.shape[0]))
            w_in = attn_w_in[j][:, perm]
        elif kind == 1:
            w_in = hgrn_w_in[j]
        elif kind == 2:
            w_in = ssd_w_in[j]
        else:
            perm = _deinterleave_perm(diff_w_in.shape[2], LANES // 2)
            n_qk = 2 * (diff_w_in.shape[2] // 3)
            perm = jnp.where(jnp.arange(perm.shape[0]) < n_qk, perm, jnp.arange(perm.shape[0]))
            w_in = diff_w_in[j][:, perm]
        w_in = w_in.astype(BF16)
        n_proj = w_in.shape[1]
        tn = 1152 if n_proj % 1024 else 1024

        proj_l = inproj(x_lat, g_rows[0], m3, 0, 0, w_in, n_proj, tn, F32)
        proj_c = inproj(x_ctx, g_rows[0], m3, b, 0, w_in, n_proj, tn, F32).reshape(b, n_ctx, n_proj)
        out_args_l = (x_lat, g_rows[1], m3, 0, 2)
        out_args_c = (x_ctx, g_rows[1], m3, b, 2)

        if kind == 0:
            x_lat, x_ctx_new = _mixer_gqa(proj_l, proj_c, attn_w_out[j].astype(BF16), attn_q_g[j], attn_k_g[j],
                                          need_ctx, out_args_l, out_args_c)
        elif kind == 1:
            cum = jnp.cumsum(jax.nn.softmax(hgrn_lb_logits.astype(F32), axis=1), axis=1)
            lb = cum[:, layer] - cum[:, 0]
            x_lat, x_ctx_new = _mixer_hgrn(proj_l, proj_c, hgrn_w_out[j].astype(BF16), lb, hgrn_out_g[j],
                                           need_ctx, out_args_l, out_args_c)
        elif kind == 2:
            x_lat, x_ctx_new = _mixer_ssd(proj_l, proj_c, ssd_w_out[j].astype(BF16), ssd_conv_w[j], ssd_conv_b[j],
                                          ssd_dt_bias[j], ssd_a_log[j], ssd_d[j], ssd_norm_g[j], need_ctx,
                                          out_args_l, out_args_c)
        else:
            lambda_init = 0.8 - 0.6 * math.exp(-0.3 * layer)
            x_lat = _mixer_diff(proj_l, proj_c, diff_w_out[j].astype(BF16), diff_lambda[j], diff_subln_g[j],
                                lambda_init, out_args_l)
            x_ctx_new = None

        w13 = ffn_w13[layer].astype(BF16)
        w2 = ffn_w2[layer].astype(BF16)
        u_l = inproj(x_lat, g_rows[2], m3, 0, 3, w13, d_ff, 512, BF16, swiglu=True)
        x_lat = outproj("plain", [u_l], [0], [], w2, x_lat, g_rows[3], m3, 0, 5)
        if need_ctx:
            x_ctx = x_ctx_new
            u_c = inproj(x_ctx, g_rows[2], m3, b, 3, w13, d_ff, 512, BF16, swiglu=True)
            x_ctx = outproj("plain", [u_c], [0], [], w2, x_ctx, g_rows[3], m3, b, 5)
    return x_lat
```
